```python
import jax, jax.numpy as jnp
from jax import lax
import numpy as np

D_MODEL = 2048
BATCH = 4
SEQ = 2048
DEPTH = 2

CTX_LEN = 256
GRID_W = 64
EPS = 1e-6

GLA_HEADS = 4
GLA_DK = 256
GLA_DV = 512
GLA_QK = GLA_HEADS * GLA_DK
GLA_VW = GLA_HEADS * GLA_DV
GLA_RANK = 16
GLA_TAU = 16.0
GLA_CHUNK = 64

MLA_HEADS = 16
MLA_Q_RANK = 512
MLA_KV_RANK = 512
MLA_NOPE = 128
MLA_ROPE = 64
MLA_V = 128
MLA_QK = MLA_NOPE + MLA_ROPE
MLA_VW = MLA_HEADS * MLA_V
ROPE_BASE = 10000.0
Q_BLOCK = 128

D_FF = -(-8 * D_MODEL // (3 * 256)) * 256

IN_SPLITS = (GLA_QK, GLA_QK, GLA_VW, GLA_VW, GLA_RANK, GLA_RANK,
             MLA_Q_RANK, MLA_KV_RANK, MLA_ROPE, D_MODEL, D_MODEL)
N_IN = sum(IN_SPLITS)
IN_OFFSETS = tuple(int(v) for v in np.cumsum(IN_SPLITS)[:-1])

kernel_name = "hybrid_gla_mla_prefix_dit"


def rmsnorm(x, g):
    xf = x.astype(jnp.float32)
    y = xf * lax.rsqrt(jnp.mean(xf * xf, axis=-1, keepdims=True) + EPS)
    return (y * g.astype(jnp.float32)).astype(x.dtype)


def modulate(h, shift, scale):
    return h * (1.0 + scale) + shift


def axial_rope_tables(n):
    rows = n // GRID_W
    row = jnp.repeat(jnp.arange(rows, dtype=jnp.float32), GRID_W)
    col = jnp.tile(jnp.arange(GRID_W, dtype=jnp.float32), rows)
    n_pairs = MLA_ROPE // 4
    freqs = ROPE_BASE ** (-jnp.arange(n_pairs, dtype=jnp.float32) / n_pairs)
    ang = jnp.concatenate([row[:, None] * freqs, col[:, None] * freqs], axis=-1)
    return jnp.cos(ang), jnp.sin(ang)


def apply_rope(x, cos, sin):
    xp = x.reshape(x.shape[:-1] + (MLA_ROPE // 2, 2))
    x0, x1 = xp[..., 0], xp[..., 1]
    cos = cos.astype(x.dtype)
    sin = sin.astype(x.dtype)
    return jnp.stack([x0 * cos - x1 * sin, x0 * sin + x1 * cos], axis=-1).reshape(x.shape)


def gla_chunked(q, k, v, log_a, s0):
    b, t, h, _ = q.shape
    dv = v.shape[-1]
    nc = t // GLA_CHUNK

    def to_chunks(z):
        z = z.astype(jnp.float32)
        return z.reshape(b, nc, GLA_CHUNK, h, z.shape[-1]).transpose(1, 0, 3, 2, 4)

    qc, kc, vc, gc = map(to_chunks, (q, k, v, log_a))
    gcum = jnp.cumsum(gc, axis=-2)
    mask = jnp.tril(jnp.ones((GLA_CHUNK, GLA_CHUNK), dtype=bool))

    def step(s, inp):
        qi, ki, vi, gi = inp
        g_last = gi[:, :, -1, :]
        q_dec = qi * jnp.exp(gi)
        k_dec = ki * jnp.exp(-gi)
        att = jnp.where(mask, jnp.einsum('bhid,bhjd->bhij', q_dec, k_dec), 0.0)
        o = jnp.einsum('bhij,bhjv->bhiv', att, vi) + jnp.einsum('bhid,bhdv->bhiv', q_dec, s)
        k_to_end = ki * jnp.exp(g_last[:, :, None, :] - gi)
        s_new = s * jnp.exp(g_last)[..., None] + jnp.einsum('bhjd,bhjv->bhdv', k_to_end, vi)
        return s_new, o

    s_fin, oc = lax.scan(step, s0, (qc, kc, vc, gcum))
    o = oc.transpose(1, 0, 3, 2, 4).reshape(b, t, h, dv)
    return o.astype(v.dtype), s_fin


def gla_bidirectional(q, k, v, la_f, la_b, s0_f, s0_b):
    o_f, s_f = gla_chunked(q, k, v, la_f, s0_f)
    o_b, s_b = gla_chunked(q[:, ::-1], k[:, ::-1], v[:, ::-1], la_b[:, ::-1], s0_b)
    return o_f + o_b[:, ::-1], s_f, s_b


def gla_features(q, k, v, gf, gb, w_up_f, b_f, w_up_b, b_b):
    b, t, _ = q.shape
    shp_k = (b, t, GLA_HEADS, GLA_DK)
    q = q.reshape(shp_k) * (GLA_DK ** -0.5)
    k = k.reshape(shp_k)
    v = v.reshape(b, t, GLA_HEADS, GLA_DV)
    la_f = jax.nn.log_sigmoid((gf @ w_up_f + b_f).astype(jnp.float32)) / GLA_TAU
    la_b = jax.nn.log_sigmoid((gb @ w_up_b + b_b).astype(jnp.float32)) / GLA_TAU
    return q, k, v, la_f.reshape(shp_k), la_b.reshape(shp_k)


def gla_output(o, r, g):
    b, t = o.shape[:2]
    return rmsnorm(o, g).reshape(b, t, GLA_VW) * jax.nn.silu(r)


def mla_features(cq, ckv, krope, g_q, w_q_up, g_kv, w_kv_up, cos=None, sin=None):
    b, t, _ = cq.shape
    q = (rmsnorm(cq, g_q) @ w_q_up).reshape(b, t, MLA_HEADS, MLA_QK)
    kv = (rmsnorm(ckv, g_kv) @ w_kv_up).reshape(b, t, MLA_HEADS, MLA_NOPE + MLA_V)
    q_nope, q_rope = q[..., :MLA_NOPE], q[..., MLA_NOPE:]
    k_nope, v = kv[..., :MLA_NOPE], kv[..., MLA_NOPE:]
    if cos is not None:
        q_rope = apply_rope(q_rope, cos[:, None, :], sin[:, None, :])
        krope = apply_rope(krope, cos, sin)
    return q_nope, q_rope, k_nope, krope, v


def mla_attend(q_nope, q_rope, k_nope, k_rope, v):
    b, n, h, _ = q_nope.shape
    nb = n // Q_BLOCK
    scale = MLA_QK ** -0.5
    qn = q_nope.reshape(b, nb, Q_BLOCK, h, MLA_NOPE).transpose(1, 0, 2, 3, 4)
    qr = q_rope.reshape(b, nb, Q_BLOCK, h, MLA_ROPE).transpose(1, 0, 2, 3, 4)

    def one_block(args):
        qn_b, qr_b = args
        s = (jnp.einsum('bqhd,bkhd->bhqk', qn_b, k_nope)
             + jnp.einsum('bqhr,bkr->bhqk', qr_b, k_rope)).astype(jnp.float32) * scale
        p = jax.nn.softmax(s, axis=-1).astype(v.dtype)
        return jnp.einsum('bhqk,bkhv->bqhv', p, v)

    o = lax.map(one_block, (qn, qr))
    return o.transpose(1, 0, 2, 3, 4).reshape(b, n, h * MLA_V)


def merge_branches(gla, mla, gate_a, gate_b, w_a, w_b, w_o):
    y = jax.nn.sigmoid(gate_a) * (gla @ w_a) + jax.nn.sigmoid(gate_b) * (mla @ w_b)
    return y @ w_o


def swiglu(h, w_gate, w_up, w_down):
    return (jax.nn.silu(h @ w_gate) * (h @ w_up)) @ w_down


def setup_inputs(seed: int = 0) -> dict:
    key = jax.random.key(seed)
    ks = jax.random.split(key, 32)
    f32 = jnp.float32
    L = DEPTH

    def nrm(k, shape, scale):
        return jax.random.normal(k, shape, f32) * scale

    def gain(k, shape):
        return 1.0 + 0.05 * jax.random.normal(k, shape, f32)

    return {
        "x": nrm(ks[0], (BATCH, SEQ, D_MODEL), 1.0),
        "c": nrm(ks[1], (BATCH, D_MODEL), 1.0),
        "ctx": nrm(ks[2], (BATCH, CTX_LEN, D_MODEL), 1.0),
        "c_ctx": nrm(ks[3], (D_MODEL,), 1.0),
        "w_mod": nrm(ks[4], (L, D_MODEL, 6 * D_MODEL), 0.5 * D_MODEL ** -0.5),
        "b_mod": nrm(ks[5], (L, 6 * D_MODEL), 0.02),
        "g_attn": gain(ks[6], (L, D_MODEL)),
        "g_ffn": gain(ks[7], (L, D_MODEL)),
        "w_in": nrm(ks[8], (L, D_MODEL, N_IN), D_MODEL ** -0.5),
        "w_gla_up_f": nrm(ks[9], (L, GLA_RANK, GLA_QK), GLA_RANK ** -0.5),
        "b_gla_f": nrm(ks[10], (L, GLA_QK), 0.1),
        "w_gla_up_b": nrm(ks[11], (L, GLA_RANK, GLA_QK), GLA_RANK ** -0.5),
        "b_gla_b": nrm(ks[12], (L, GLA_QK), 0.1),
        "g_gla_out": gain(ks[13], (L, GLA_DV)),
        "g_q_lora": gain(ks[14], (L, MLA_Q_RANK)),
        "w_q_up": nrm(ks[15], (L, MLA_Q_RANK, MLA_HEADS * MLA_QK), MLA_Q_RANK ** -0.5),
        "g_kv_lora": gain(ks[16], (L, MLA_KV_RANK)),
        "w_kv_up": nrm(ks[17], (L, MLA_KV_RANK, MLA_HEADS * (MLA_NOPE + MLA_V)), MLA_KV_RANK ** -0.5),
        "w_branch_a": nrm(ks[18], (L, GLA_VW, D_MODEL), GLA_VW ** -0.5),
        "w_branch_b": nrm(ks[19], (L, MLA_VW, D_MODEL), MLA_VW ** -0.5),
        "w_out": nrm(ks[20], (L, D_MODEL, D_MODEL), D_MODEL ** -0.5),
        "w_ffn_gate": nrm(ks[21], (L, D_MODEL, D_FF), D_MODEL ** -0.5),
        "w_ffn_up": nrm(ks[22], (L, D_MODEL, D_FF), D_MODEL ** -0.5),
        "w_ffn_down": nrm(ks[23], (L, D_FF, D_MODEL), D_FF ** -0.5),
        "g_final": gain(ks[24], (D_MODEL,)),
    }


def reference(x, c, ctx, c_ctx, w_mod, b_mod, g_attn, g_ffn, w_in,
              w_gla_up_f, b_gla_f, w_gla_up_b, b_gla_b, g_gla_out,
              g_q_lora, w_q_up, g_kv_lora, w_kv_up,
              w_branch_a, w_branch_b, w_out,
              w_ffn_gate, w_ffn_up, w_ffn_down, g_final):
    b, n, _ = x.shape
    cos, sin = axial_rope_tables(n)
    s_c = ctx
    s0 = jnp.zeros((b, GLA_HEADS, GLA_DK, GLA_DV), jnp.float32)

    for l in range(DEPTH):
        last = l == DEPTH - 1
        mod_x = (jax.nn.silu(c) @ w_mod[l] + b_mod[l])[:, None, :]
        mod_c = jax.nn.silu(c_ctx) @ w_mod[l] + b_mod[l]
        sa_x, ca_x, ga_x, sf_x, cf_x, gf_x = jnp.split(mod_x, 6, axis=-1)
        sa_c, ca_c, ga_c, sf_c, cf_c, gf_c = jnp.split(mod_c, 6, axis=-1)

        hx = modulate(rmsnorm(x, g_attn[l]), sa_x, ca_x)
        hc = modulate(rmsnorm(s_c, g_attn[l]), sa_c, ca_c)
        (qx, kx, vx, rx, gfx, gbx, cqx, ckvx, krx, bgax, bgbx) = jnp.split(hx @ w_in[l], IN_OFFSETS, axis=-1)
        (qc, kc, vc, rc, gfc, gbc, cqc, ckvc, krc, bgac, bgbc) = jnp.split(hc @ w_in[l], IN_OFFSETS, axis=-1)

        feats_c = gla_features(qc, kc, vc, gfc, gbc, w_gla_up_f[l], b_gla_f[l], w_gla_up_b[l], b_gla_b[l])
        feats_x = gla_features(qx, kx, vx, gfx, gbx, w_gla_up_f[l], b_gla_f[l], w_gla_up_b[l], b_gla_b[l])
        o_c, st_f, st_b = gla_bidirectional(*feats_c, s0, s0)
        o_x, _, _ = gla_bidirectional(*feats_x, st_f, st_b)
        gla_x = gla_output(o_x, rx, g_gla_out[l])

        qn_c, qr_c, kn_c, kr_c, v_c = mla_features(cqc, ckvc, krc, g_q_lora[l], w_q_up[l], g_kv_lora[l], w_kv_up[l])
        qn_x, qr_x, kn_x, kr_x, v_x = mla_features(cqx, ckvx, krx, g_q_lora[l], w_q_up[l], g_kv_lora[l], w_kv_up[l], cos, sin)
        mla_x = mla_attend(qn_x, qr_x,
                           jnp.concatenate([kn_c, kn_x], axis=1),
                           jnp.concatenate([kr_c, kr_x], axis=1),
                           jnp.concatenate([v_c, v_x], axis=1))

        out_x = merge_branches(gla_x, mla_x, bgax, bgbx, w_branch_a[l], w_branch_b[l], w_out[l])
        x = x + ga_x * out_x
        if not last:
            gla_c = gla_output(o_c, rc, g_gla_out[l])
            mla_c = mla_attend(qn_c, qr_c, kn_c, kr_c, v_c)
            out_c = merge_branches(gla_c, mla_c, bgac, bgbc, w_branch_a[l], w_branch_b[l], w_out[l])
            s_c = s_c + ga_c * out_c

        hx = modulate(rmsnorm(x, g_ffn[l]), sf_x, cf_x)
        x = x + gf_x * swiglu(hx, w_ffn_gate[l], w_ffn_up[l], w_ffn_down[l])
        if not last:
            hc = modulate(rmsnorm(s_c, g_ffn[l]), sf_c, cf_c)
            s_c = s_c + gf_c * swiglu(hc, w_ffn_gate[l], w_ffn_up[l], w_ffn_down[l])

    return rmsnorm(x, g_final)
```

```python
import functools

import jax
import jax.numpy as jnp
from jax import lax
from jax.experimental import pallas as pl
from jax.experimental.pallas import tpu as pltpu

F32 = jnp.float32
BF16 = jnp.bfloat16

D_MODEL = 2048
BATCH = 4
SEQ = 2048
DEPTH = 2
CTX_LEN = 256
GRID_W = 64
EPS = 1e-6

GLA_HEADS = 4
GLA_DK = 256
GLA_DV = 512
GLA_QK = GLA_HEADS * GLA_DK
GLA_VW = GLA_HEADS * GLA_DV
GLA_RANK = 16
GLA_TAU = 16.0
GLA_CHUNK = 64

MLA_HEADS = 16
MLA_Q_RANK = 512
MLA_KV_RANK = 512
MLA_NOPE = 128
MLA_ROPE = 64
MLA_V = 128
MLA_QK = MLA_NOPE + MLA_ROPE
ROPE_BASE = 10000.0
D_FF = 5632

COL_Q = 0
COL_K = COL_Q + GLA_QK
COL_V = COL_K + GLA_QK
COL_R = COL_V + GLA_VW
COL_GA = COL_R + GLA_VW
COL_GB = COL_GA + D_MODEL
COL_CQ = COL_GB + D_MODEL
COL_CKV = COL_CQ + MLA_Q_RANK
N_MAIN = COL_CKV + MLA_KV_RANK
N_SMALL = 256
LANES = 128
MLA_HEAD_W = 256

V7X_VMEM_LIMIT = 56 * 1024 * 1024

NT = (((1,), (1,)), ((), ()))
TN = (((0,), (0,)), ((), ()))


def _params(*sem):
    return pltpu.CompilerParams(dimension_semantics=sem, vmem_limit_bytes=V7X_VMEM_LIMIT)


def _dot(a, b):
    return jnp.dot(a, b, preferred_element_type=F32)


def _rms(x, g):
    return x * lax.rsqrt(jnp.mean(x * x, axis=-1, keepdims=True) + EPS) * g


def _silu(x):
    return x * jax.nn.sigmoid(x)


def _mod_kernel(cs_ref, w_ref, b_ref, o_ref):
    h = _silu(cs_ref[...]).astype(BF16)
    o_ref[...] = _dot(h, w_ref[...].astype(BF16)) + b_ref[...]


def _modulation(cs, w_mod, b_mod):
    tn = D_MODEL
    return pl.pallas_call(
        _mod_kernel,
        out_shape=jax.ShapeDtypeStruct((DEPTH, 8, 6 * D_MODEL), F32),
        grid=(DEPTH, 6 * D_MODEL // tn),
        in_specs=[
            pl.BlockSpec((8, D_MODEL), lambda l, j: (0, 0)),
            pl.BlockSpec((None, D_MODEL, tn), lambda l, j: (l, 0, j)),
            pl.BlockSpec((None, 1, tn), lambda l, j: (l, 0, j)),
        ],
        out_specs=pl.BlockSpec((None, 8, tn), lambda l, j: (l, 0, j)),
        compiler_params=_params("parallel", "parallel"),
        name="adaln_mod",
    )(cs, w_mod, b_mod.reshape(DEPTH, 1, 6 * D_MODEL))


def _mod_spec(which, mod_row, tn=D_MODEL, col=False):
    if col:
        return pl.BlockSpec((None, None, 1, tn), lambda i, j: (mod_row(i), which, 0, j))
    return pl.BlockSpec((None, None, 1, tn), lambda i, j: (mod_row(i), which, 0, 0))


def _inproj_kernel(x_ref, g_ref, sh_ref, sc_ref, w_ref, ws_ref, p_ref, ps_ref, h_ref):
    @pl.when(pl.program_id(1) == 0)
    def _():
        h = _rms(x_ref[...], g_ref[...]) * (1.0 + sc_ref[...]) + sh_ref[...]
        h = h.astype(BF16)
        h_ref[...] = h
        ps_ref[...] = _dot(h, ws_ref[...])

    p_ref[...] = _dot(h_ref[...], w_ref[...])


def _inproj(x, g, mod, mod_row, w_main, w_small, tm, tn):
    m = x.shape[0]
    return pl.pallas_call(
        _inproj_kernel,
        out_shape=[jax.ShapeDtypeStruct((m, N_MAIN), F32), jax.ShapeDtypeStruct((m, N_SMALL), F32)],
        grid=(m // tm, N_MAIN // tn),
        in_specs=[
            pl.BlockSpec((tm, D_MODEL), lambda i, j: (i, 0)),
            pl.BlockSpec((1, D_MODEL), lambda i, j: (0, 0)),
            _mod_spec(0, mod_row),
            _mod_spec(1, mod_row),
            pl.BlockSpec((D_MODEL, tn), lambda i, j: (0, j)),
            pl.BlockSpec((D_MODEL, N_SMALL), lambda i, j: (0, 0)),
        ],
        out_specs=[
            pl.BlockSpec((tm, tn), lambda i, j: (i, j)),
            pl.BlockSpec((tm, N_SMALL), lambda i, j: (i, 0)),
        ],
        scratch_shapes=[pltpu.VMEM((tm, D_MODEL), BF16)],
        compiler_params=_params("parallel", "arbitrary"),
        name="in_proj",
    )(x, g, mod, mod, w_main, w_small)


GLA_GROUP = 256


def _log_sigmoid(z):
    return -(jnp.maximum(-z, 0.0) + jnp.log1p(jnp.exp(-jnp.abs(z))))


def _gla_kernel(qc, kc, vc, rc, sc, ql, kl, vl, rl, sl, wf, wb, bf, bb, gout,
                oc_ref, ol_ref, gf_s, gb_s, oacc, st):
    n_ctx, n_lat = qc.shape[0], ql.shape[0]

    r = lax.broadcasted_iota(jnp.int32, (GLA_GROUP, GLA_GROUP), 0)
    c = lax.broadcasted_iota(jnp.int32, (GLA_GROUP, GLA_GROUP), 1)
    shift = GLA_CHUNK.bit_length() - 1
    same = jnp.right_shift(r, shift) == jnp.right_shift(c, shift)
    tril = jnp.where(same & (r >= c), 1.0, 0.0).astype(F32)
    triu = jnp.where(same & (r <= c), 1.0, 0.0).astype(F32)

    def gates(s_blk, dst):
        sb = s_blk.astype(BF16)
        la_f = _log_sigmoid(_dot(sb, wf[...]) + bf[...]) * (1.0 / GLA_TAU)
        la_b = _log_sigmoid(_dot(sb, wb[...]) + bb[...]) * (1.0 / GLA_TAU)
        gf_s[pl.ds(dst, GLA_GROUP), :] = jnp.dot(
            tril, la_f, precision=lax.Precision.HIGHEST, preferred_element_type=F32)
        gb_s[pl.ds(dst, GLA_GROUP), :] = jnp.dot(
            triu, la_b, precision=lax.Precision.HIGHEST, preferred_element_type=F32)

    for i in range(n_ctx // GLA_GROUP):
        gates(sc[pl.ds(i * GLA_GROUP, GLA_GROUP), :], i * GLA_GROUP)

    def lat_gates(i, carry):
        r0 = pl.multiple_of(i * GLA_GROUP, GLA_GROUP)
        gates(sl[pl.ds(r0, GLA_GROUP), :], n_ctx + r0)
        return carry

    lax.fori_loop(0, n_lat // GLA_GROUP, lat_gates, 0)

    cr = lax.broadcasted_iota(jnp.int32, (GLA_CHUNK, GLA_CHUNK), 0)
    cc = lax.broadcasted_iota(jnp.int32, (GLA_CHUNK, GLA_CHUNK), 1)

    def chunk(q_ref, k_ref, v_ref, r_ref, o_ref, r0, g0, fwd):
        g_s = gf_s if fwd else gb_s
        rows = pl.ds(r0, GLA_CHUNK)
        q = q_ref[rows, :] * (GLA_DK ** -0.5)
        k = k_ref[rows, :]
        v = v_ref[rows, :].astype(BF16)
        g = g_s[pl.ds(g0, GLA_CHUNK), :]
        g_last = g[GLA_CHUNK - 1:GLA_CHUNK, :] if fwd else g[0:1, :]
        q_dec = (q * jnp.exp(g)).astype(BF16)
        k_dec = (k * jnp.exp(-g)).astype(BF16)
        k_end = (k * jnp.exp(g_last - g)).astype(BF16)
        att = lax.dot_general(q_dec, k_dec, NT, preferred_element_type=F32)
        att = jnp.where((cr >= cc) if fwd else (cr <= cc), att, 0.0).astype(BF16)
        s_prev = st[...]
        o = _dot(att, v) + lax.dot_general(q_dec, s_prev.astype(BF16), NT, preferred_element_type=F32)
        st[...] = s_prev * jnp.exp(g_last) + lax.dot_general(v, k_end, TN, preferred_element_type=F32)
        acc_rows = pl.ds(g0, GLA_CHUNK)
        if fwd:
            oacc[acc_rows, :] = o
        else:
            o = oacc[acc_rows, :] + o
            y = _rms(o, gout[...])
            o_ref[rows, :] = (y * _silu(r_ref[rows, :])).astype(o_ref.dtype)

    def run(fwd):
        st[...] = jnp.zeros_like(st)
        for refs, n, base in (((qc, kc, vc, rc, oc_ref), n_ctx, 0), ((ql, kl, vl, rl, ol_ref), n_lat, n_ctx)):
            nch = n // GLA_CHUNK

            def body(i, carry, refs=refs, nch=nch, base=base):
                ci = i if fwd else nch - 1 - i
                r0 = pl.multiple_of(ci * GLA_CHUNK, GLA_CHUNK)
                chunk(*refs, r0, pl.multiple_of(base + r0, GLA_CHUNK), fwd)
                return carry

            lax.fori_loop(0, nch, body, 0)

    run(True)
    run(False)


def _gla(p_ctx, s_ctx, p_lat, s_lat, wf, wb, bf, bb, gout):
    def stream(n):
        return [
            pl.BlockSpec((n, GLA_DK), lambda b, h: (b, COL_Q // GLA_DK + h)),
            pl.BlockSpec((n, GLA_DK), lambda b, h: (b, COL_K // GLA_DK + h)),
            pl.BlockSpec((n, GLA_DV), lambda b, h: (b, COL_V // GLA_DV + h)),
            pl.BlockSpec((n, GLA_DV), lambda b, h: (b, COL_R // GLA_DV + h)),
            pl.BlockSpec((n, LANES), lambda b, h: (b, 0)),
        ]

    n_all = CTX_LEN + SEQ
    return pl.pallas_call(
        _gla_kernel,
        out_shape=[jax.ShapeDtypeStruct((BATCH * CTX_LEN, GLA_VW), BF16),
                   jax.ShapeDtypeStruct((BATCH * SEQ, GLA_VW), BF16)],
        grid=(BATCH, GLA_HEADS),
        in_specs=stream(CTX_LEN) + stream(SEQ) + [
            pl.BlockSpec((LANES, GLA_DK), lambda b, h: (0, h)),
            pl.BlockSpec((LANES, GLA_DK), lambda b, h: (0, h)),
            pl.BlockSpec((1, GLA_DK), lambda b, h: (0, h)),
            pl.BlockSpec((1, GLA_DK), lambda b, h: (0, h)),
            pl.BlockSpec((1, GLA_DV), lambda b, h: (0, 0)),
        ],
        out_specs=[pl.BlockSpec((CTX_LEN, GLA_DV), lambda b, h: (b, h)),
                   pl.BlockSpec((SEQ, GLA_DV), lambda b, h: (b, h))],
        scratch_shapes=[
            pltpu.VMEM((n_all, GLA_DK), F32),
            pltpu.VMEM((n_all, GLA_DK), F32),
            pltpu.VMEM((n_all, GLA_DV), F32),
            pltpu.VMEM((GLA_DV, GLA_DK), F32),
        ],
        compiler_params=_params("parallel", "parallel"),
        name="gla_bidir",
    )(p_ctx, p_ctx, p_ctx, p_ctx, s_ctx, p_lat, p_lat, p_lat, p_lat, s_lat, wf, wb, bf, bb, gout)


def _rope(x, cos, sin):
    return x * cos + pltpu.roll(x, MLA_ROPE // 2, 1) * sin


def _mla_up_kernel(cq_ref, ckv_ref, ps_ref, cos_ref, sin_ref, gq_ref, gkv_ref, wq_ref, wkv_ref,
                   q_out, k_out, v_out):
    cqn = _rms(cq_ref[...], gq_ref[...]).astype(BF16)
    ckvn = _rms(ckv_ref[...], gkv_ref[...]).astype(BF16)
    cos, sin = cos_ref[...], sin_ref[...]
    k_rope = _rope(ps_ref[...], cos, sin).astype(BF16)
    for h in range(MLA_HEADS):
        cols = slice(h * MLA_HEAD_W, (h + 1) * MLA_HEAD_W)
        qh = _dot(cqn, wq_ref[:, cols])
        q_out[h, :, 0:MLA_NOPE] = qh[:, :MLA_NOPE].astype(BF16)
        q_out[h, :, MLA_NOPE:] = _rope(qh[:, MLA_NOPE:], cos, sin).astype(BF16)
        kvh = _dot(ckvn, wkv_ref[:, cols])
        k_out[h, :, 0:MLA_NOPE] = kvh[:, :MLA_NOPE].astype(BF16)
        k_out[h, :, MLA_NOPE:] = k_rope
        v_out[h] = kvh[:, MLA_NOPE:].astype(BF16)


def _mla_up(p, ps, cos, sin, gq, gkv, wq, wkv, n_pos, tm):
    nt = n_pos // tm
    row = lambda b, i: b * nt + i
    head_spec = lambda w: pl.BlockSpec((None, MLA_HEADS, tm, w), lambda b, i: (b, 0, i, 0))
    return pl.pallas_call(
        _mla_up_kernel,
        out_shape=[jax.ShapeDtypeStruct((BATCH, MLA_HEADS, n_pos, MLA_HEAD_W), BF16),
                   jax.ShapeDtypeStruct((BATCH, MLA_HEADS, n_pos, MLA_HEAD_W), BF16),
                   jax.ShapeDtypeStruct((BATCH, MLA_HEADS, n_pos, MLA_V), BF16)],
        grid=(BATCH, nt),
        in_specs=[
            pl.BlockSpec((tm, MLA_Q_RANK), lambda b, i: (row(b, i), COL_CQ // MLA_Q_RANK)),
            pl.BlockSpec((tm, MLA_KV_RANK), lambda b, i: (row(b, i), COL_CKV // MLA_KV_RANK)),
            pl.BlockSpec((tm, LANES), lambda b, i: (row(b, i), 1)),
            pl.BlockSpec((tm, LANES), lambda b, i: (i, 0)),
            pl.BlockSpec((tm, LANES), lambda b, i: (i, 0)),
            pl.BlockSpec((1, MLA_Q_RANK), lambda b, i: (0, 0)),
            pl.BlockSpec((1, MLA_KV_RANK), lambda b, i: (0, 0)),
            pl.BlockSpec((MLA_Q_RANK, MLA_HEADS * MLA_HEAD_W), lambda b, i: (0, 0)),
            pl.BlockSpec((MLA_KV_RANK, MLA_HEADS * MLA_HEAD_W), lambda b, i: (0, 0)),
        ],
        out_specs=[head_spec(MLA_HEAD_W), head_spec(MLA_HEAD_W), head_spec(MLA_V)],
        compiler_params=_params("parallel", "parallel"),
        name="mla_up",
    )(p, p, ps, cos, sin, gq, gkv, wq, wkv)


ATTN_HEADS_PER_STEP = 2


def _attn_kernel(q_ref, *refs, n_kv):
    k_refs, v_refs, o_ref = refs[:n_kv], refs[n_kv:2 * n_kv], refs[2 * n_kv]
    scale = MLA_QK ** -0.5
    for h in range(ATTN_HEADS_PER_STEP):
        q = q_ref[h]
        s = [lax.dot_general(q, k[h], NT, preferred_element_type=F32) * scale for k in k_refs]
        m = functools.reduce(jnp.maximum, [jnp.max(x, axis=-1, keepdims=True) for x in s])
        p = [jnp.exp(x - m) for x in s]
        denom = functools.reduce(jnp.add, [jnp.sum(x, axis=-1, keepdims=True) for x in p])
        o = functools.reduce(jnp.add, [_dot(x.astype(BF16), v[h]) for x, v in zip(p, v_refs)])
        o_ref[:, h * MLA_V:(h + 1) * MLA_V] = (o / denom).astype(o_ref.dtype)


def _attention(q, ks, vs, tq):
    nq = q.shape[2]
    nt = nq // tq
    hb = ATTN_HEADS_PER_STEP
    kv_spec = lambda a: pl.BlockSpec((None, hb) + a.shape[2:], lambda b, g, i: (b, g, 0, 0))
    return pl.pallas_call(
        functools.partial(_attn_kernel, n_kv=len(ks)),
        out_shape=jax.ShapeDtypeStruct((BATCH * nq, MLA_HEADS * MLA_V), BF16),
        grid=(BATCH, MLA_HEADS // hb, nt),
        in_specs=[pl.BlockSpec((None, hb, tq, MLA_HEAD_W), lambda b, g, i: (b, g, i, 0))]
        + [kv_spec(a) for a in ks] + [kv_spec(a) for a in vs],
        out_specs=pl.BlockSpec((tq, hb * MLA_V), lambda b, g, i: (b * nt + i, g)),
        compiler_params=_params("parallel", "parallel", "parallel"),
        name="mla_attn",
    )(q, *ks, *vs)


def _merge_kernel(gla_ref, mla_ref, ga_ref, gb_ref, wa_ref, wb_ref, y_ref):
    y = (jax.nn.sigmoid(ga_ref[...]) * _dot(gla_ref[...], wa_ref[...])
         + jax.nn.sigmoid(gb_ref[...]) * _dot(mla_ref[...], wb_ref[...]))
    y_ref[...] = y.astype(y_ref.dtype)


def _merge(gla, mla, p, wa, wb, tm, tn):
    m = gla.shape[0]
    return pl.pallas_call(
        _merge_kernel,
        out_shape=jax.ShapeDtypeStruct((m, D_MODEL), BF16),
        grid=(m // tm, D_MODEL // tn),
        in_specs=[
            pl.BlockSpec((tm, GLA_VW), lambda i, j: (i, 0)),
            pl.BlockSpec((tm, MLA_HEADS * MLA_V), lambda i, j: (i, 0)),
            pl.BlockSpec((tm, tn), lambda i, j: (i, COL_GA // tn + j)),
            pl.BlockSpec((tm, tn), lambda i, j: (i, COL_GB // tn + j)),
            pl.BlockSpec((GLA_VW, tn), lambda i, j: (0, j)),
            pl.BlockSpec((MLA_HEADS * MLA_V, tn), lambda i, j: (0, j)),
        ],
        out_specs=pl.BlockSpec((tm, tn), lambda i, j: (i, j)),
        compiler_params=_params("parallel", "parallel"),
        name="branch_merge",
    )(gla, mla, p, p, wa, wb)


def _proj_residual_kernel(y_ref, w_ref, x_ref, gate_ref, o_ref):
    o_ref[...] = x_ref[...] + gate_ref[...] * _dot(y_ref[...], w_ref[...])


def _proj_residual(y, w, x, mod, which, mod_row, tm, tn):
    m, kdim = y.shape
    return pl.pallas_call(
        _proj_residual_kernel,
        out_shape=jax.ShapeDtypeStruct((m, D_MODEL), F32),
        grid=(m // tm, D_MODEL // tn),
        in_specs=[
            pl.BlockSpec((tm, kdim), lambda i, j: (i, 0)),
            pl.BlockSpec((kdim, tn), lambda i, j: (0, j)),
            pl.BlockSpec((tm, tn), lambda i, j: (i, j)),
            _mod_spec(which, mod_row, tn, col=True),
        ],
        out_specs=pl.BlockSpec((tm, tn), lambda i, j: (i, j)),
        compiler_params=_params("parallel", "parallel"),
        name="proj_residual",
    )(y, w, x, mod)


def _ffn_up_kernel(x_ref, g_ref, sh_ref, sc_ref, wg_ref, wu_ref, o_ref, h_ref):
    @pl.when(pl.program_id(1) == 0)
    def _():
        h = _rms(x_ref[...], g_ref[...]) * (1.0 + sc_ref[...]) + sh_ref[...]
        h_ref[...] = h.astype(BF16)

    h = h_ref[...]
    o_ref[...] = (_silu(_dot(h, wg_ref[...])) * _dot(h, wu_ref[...])).astype(o_ref.dtype)


def _ffn_up(x, g, mod, mod_row, wg, wu, tm, tn):
    m = x.shape[0]
    return pl.pallas_call(
        _ffn_up_kernel,
        out_shape=jax.ShapeDtypeStruct((m, D_FF), BF16),
        grid=(m // tm, D_FF // tn),
        in_specs=[
            pl.BlockSpec((tm, D_MODEL), lambda i, j: (i, 0)),
            pl.BlockSpec((1, D_MODEL), lambda i, j: (0, 0)),
            _mod_spec(3, mod_row),
            _mod_spec(4, mod_row),
            pl.BlockSpec((D_MODEL, tn), lambda i, j: (0, j)),
            pl.BlockSpec((D_MODEL, tn), lambda i, j: (0, j)),
        ],
        out_specs=pl.BlockSpec((tm, tn), lambda i, j: (i, j)),
        scratch_shapes=[pltpu.VMEM((tm, D_MODEL), BF16)],
        compiler_params=_params("parallel", "arbitrary"),
        name="ffn_up",
    )(x, g, mod, mod, wg, wu)


def _final_norm_kernel(x_ref, g_ref, o_ref):
    o_ref[...] = _rms(x_ref[...], g_ref[...])


def _final_norm(x, g, tm):
    m = x.shape[0]
    return pl.pallas_call(
        _final_norm_kernel,
        out_shape=jax.ShapeDtypeStruct((m, D_MODEL), F32),
        grid=(m // tm,),
        in_specs=[pl.BlockSpec((tm, D_MODEL), lambda i: (i, 0)),
                  pl.BlockSpec((1, D_MODEL), lambda i: (0, 0))],
        out_specs=pl.BlockSpec((tm, D_MODEL), lambda i: (i, 0)),
        compiler_params=_params("parallel"),
        name="final_norm",
    )(x, g)


def _deinterleave(w):
    x0, x1 = w[..., 0::2], w[..., 1::2]
    return jnp.concatenate([x0, x1, x0, x1], axis=-1)


def _rope_tables():
    rows = SEQ // GRID_W
    row = jnp.repeat(jnp.arange(rows, dtype=F32), GRID_W)
    col = jnp.tile(jnp.arange(GRID_W, dtype=F32), rows)
    n_pairs = MLA_ROPE // 4
    freqs = ROPE_BASE ** (-jnp.arange(n_pairs, dtype=F32) / n_pairs)
    ang = jnp.concatenate([row[:, None] * freqs, col[:, None] * freqs], axis=-1)
    cos, sin = jnp.cos(ang), jnp.sin(ang)
    pad = jnp.zeros((SEQ, LANES - MLA_ROPE), F32)
    cos_lat = jnp.concatenate([cos, cos, pad], axis=-1)
    sin_lat = jnp.concatenate([-sin, sin, pad], axis=-1)
    cos_ctx = jnp.concatenate([jnp.ones((CTX_LEN, MLA_ROPE), F32), jnp.zeros((CTX_LEN, LANES - MLA_ROPE), F32)], -1)
    sin_ctx = jnp.zeros((CTX_LEN, LANES), F32)
    return cos_lat, sin_lat, cos_ctx, sin_ctx


def kernel(x, c, ctx, c_ctx, w_mod, b_mod, g_attn, g_ffn, w_in, w_gla_up_f, b_gla_f, w_gla_up_b, b_gla_b, g_gla_out, g_q_lora, w_q_up, g_kv_lora, w_kv_up, w_branch_a, w_branch_b, w_out, w_ffn_gate, w_ffn_up, w_ffn_down, g_final):
    L = DEPTH
    o_gf = COL_R + GLA_VW
    o_cq = o_gf + 2 * GLA_RANK
    o_kr = o_cq + MLA_Q_RANK + MLA_KV_RANK
    o_ga = o_kr + MLA_ROPE
    w_main = jnp.concatenate(
        [w_in[:, :, :o_gf], w_in[:, :, o_ga:], w_in[:, :, o_cq:o_kr]], axis=-1).astype(BF16)
    w_small = jnp.concatenate(
        [w_in[:, :, o_gf:o_cq], jnp.zeros((L, D_MODEL, LANES - 2 * GLA_RANK), F32),
         _deinterleave(w_in[:, :, o_kr:o_ga])], axis=-1).astype(BF16)
    wq = w_q_up.reshape(L, MLA_Q_RANK, MLA_HEADS, MLA_QK)
    wq = jnp.concatenate([wq[..., :MLA_NOPE], _deinterleave(wq[..., MLA_NOPE:])], axis=-1)
    wq = wq.reshape(L, MLA_Q_RANK, MLA_HEADS * MLA_HEAD_W).astype(BF16)
    wkv = w_kv_up.astype(BF16)
    zpad = lambda n: jnp.zeros((L, n, GLA_QK), F32)
    wup_f = jnp.concatenate([w_gla_up_f, zpad(LANES - GLA_RANK)], axis=1).astype(BF16)
    wup_b = jnp.concatenate([zpad(GLA_RANK), w_gla_up_b, zpad(LANES - 2 * GLA_RANK)], axis=1).astype(BF16)
    wa, wb, wo = w_branch_a.astype(BF16), w_branch_b.astype(BF16), w_out.astype(BF16)
    wg, wu, wd = w_ffn_gate.astype(BF16), w_ffn_up.astype(BF16), w_ffn_down.astype(BF16)
    cos_lat, sin_lat, cos_ctx, sin_ctx = _rope_tables()

    cs = jnp.concatenate([c, c_ctx[None, :], jnp.zeros((8 - BATCH - 1, D_MODEL), F32)], axis=0)
    mods = _modulation(cs, w_mod, b_mod)

    xl = x.reshape(BATCH * SEQ, D_MODEL)
    xc = ctx.reshape(BATCH * CTX_LEN, D_MODEL)
    tm = 1024
    lat_row = lambda i: i // (SEQ // tm)
    ctx_row = lambda i: BATCH
    row2 = lambda v: v.reshape(1, -1)

    for l in range(L):
        last = l == L - 1
        mod = mods[l].reshape(8, 6, 1, D_MODEL)
        p_lat, s_lat = _inproj(xl, row2(g_attn[l]), mod, lat_row, w_main[l], w_small[l], tm, 1024)
        p_ctx, s_ctx = _inproj(xc, row2(g_attn[l]), mod, ctx_row, w_main[l], w_small[l], tm, 1024)

        gla_ctx, gla_lat = _gla(p_ctx, s_ctx, p_lat, s_lat, wup_f[l], wup_b[l],
                                row2(b_gla_f[l]), row2(b_gla_b[l]), row2(g_gla_out[l]))

        gq, gkv = row2(g_q_lora[l]), row2(g_kv_lora[l])
        q_lat, k_lat, v_lat = _mla_up(p_lat, s_lat, cos_lat, sin_lat, gq, gkv, wq[l], wkv[l], SEQ, 512)
        q_ctx, k_ctx, v_ctx = _mla_up(p_ctx, s_ctx, cos_ctx, sin_ctx, gq, gkv, wq[l], wkv[l], CTX_LEN, CTX_LEN)
        mla_lat = _attention(q_lat, [k_ctx, k_lat], [v_ctx, v_lat], 512)

        y = _merge(gla_lat, mla_lat, p_lat, wa[l], wb[l], tm, 512)
        xl = _proj_residual(y, wo[l], xl, mod, 2, lat_row, tm, 512)
        if not last:
            mla_ctx = _attention(q_ctx, [k_ctx], [v_ctx], CTX_LEN)
            y = _merge(gla_ctx, mla_ctx, p_ctx, wa[l], wb[l], tm, 512)
            xc = _proj_residual(y, wo[l], xc, mod, 2, ctx_row, tm, 512)

        hid = _ffn_up(xl, row2(g_ffn[l]), mod, lat_row, wg[l], wu[l], tm, 512)
        xl = _proj_residual(hid, wd[l], xl, mod, 5, lat_row, tm, 512)
        if not last:
            hid = _ffn_up(xc, row2(g_ffn[l]), mod, ctx_row, wg[l], wu[l], tm, 512)
            xc = _proj_residual(hid, wd[l], xc, mod, 5, ctx_row, tm, 512)

    return _final_norm(xl, row2(g_final), 512).reshape(BATCH, SEQ, D_MODEL)
```

```python
import functools

import jax
import jax.numpy as jnp
from jax import lax
from jax.experimental import pallas as pl
from jax.experimental.pallas import tpu as pltpu

F32 = jnp.float32
BF16 = jnp.bfloat16

D_MODEL = 2048
BATCH = 4
SEQ = 2048
DEPTH = 2
CTX_LEN = 256
GRID_W = 64
EPS = 1e-6

GLA_HEADS = 4
GLA_DK = 256
GLA_DV = 512
GLA_QK = GLA_HEADS * GLA_DK
GLA_VW = GLA_HEADS * GLA_DV
GLA_RANK = 16
GLA_TAU = 16.0
GLA_CHUNK = 64

MLA_HEADS = 16
MLA_Q_RANK = 512
MLA_KV_RANK = 512
MLA_NOPE = 128
MLA_ROPE = 64
MLA_V = 128
MLA_QK = MLA_NOPE + MLA_ROPE
ROPE_BASE = 10000.0
D_FF = 5632

COL_Q = 0
COL_K = COL_Q + GLA_QK
COL_V = COL_K + GLA_QK
COL_R = COL_V + GLA_VW
COL_GA = COL_R + GLA_VW
COL_GB = COL_GA + D_MODEL
COL_CQ = COL_GB + D_MODEL
COL_CKV = COL_CQ + MLA_Q_RANK
N_MAIN = COL_CKV + MLA_KV_RANK
N_SMALL = 256
LANES = 128
MLA_HEAD_W = 256
SCORE_SCALE_LOG2 = (MLA_QK ** -0.5) * 1.4426950408889634

V7X_VMEM_LIMIT = 56 * 1024 * 1024

NT = (((1,), (1,)), ((), ()))
TN = (((0,), (0,)), ((), ()))


def _params(*sem):
    return pltpu.CompilerParams(dimension_semantics=sem, vmem_limit_bytes=V7X_VMEM_LIMIT)


def _dot(a, b):
    return jnp.dot(a, b, preferred_element_type=F32)


def _rms(x, g):
    return x * lax.rsqrt(jnp.mean(x * x, axis=-1, keepdims=True) + EPS) * g


def _silu(x):
    return x * jax.nn.sigmoid(x)


def _mod_kernel(cs_ref, w_ref, b_ref, o_ref):
    h = _silu(cs_ref[...]).astype(BF16)
    o_ref[...] = _dot(h, w_ref[...].astype(BF16)) + b_ref[...]


def _modulation(cs, w_mod, b_mod):
    tn = D_MODEL
    return pl.pallas_call(
        _mod_kernel,
        out_shape=jax.ShapeDtypeStruct((DEPTH, 8, 6 * D_MODEL), F32),
        grid=(DEPTH, 6 * D_MODEL // tn),
        in_specs=[
            pl.BlockSpec((8, D_MODEL), lambda l, j: (0, 0)),
            pl.BlockSpec((None, D_MODEL, tn), lambda l, j: (l, 0, j)),
            pl.BlockSpec((None, 1, tn), lambda l, j: (l, 0, j)),
        ],
        out_specs=pl.BlockSpec((None, 8, tn), lambda l, j: (l, 0, j)),
        compiler_params=_params("parallel", "parallel"),
        name="adaln_mod",
    )(cs, w_mod, b_mod.reshape(DEPTH, 1, 6 * D_MODEL))


def _mod_spec(which, mod_row, tn=D_MODEL, col=False):
    if col:
        return pl.BlockSpec((None, None, 1, tn), lambda i, j: (mod_row(i), which, 0, j))
    return pl.BlockSpec((None, None, 1, tn), lambda i, j: (mod_row(i), which, 0, 0))


def _inproj_kernel(x_ref, g_ref, sh_ref, sc_ref, w_ref, ws_ref, p_ref, ps_ref, h_ref):
    @pl.when(pl.program_id(1) == 0)
    def _():
        h = _rms(x_ref[...], g_ref[...]) * (1.0 + sc_ref[...]) + sh_ref[...]
        h = h.astype(BF16)
        h_ref[...] = h
        ps_ref[...] = _dot(h, ws_ref[...])

    p_ref[...] = _dot(h_ref[...], w_ref[...])


def _inproj(x, g, mod, mod_row, w_main, w_small, tm, tn):
    m = x.shape[0]
    return pl.pallas_call(
        _inproj_kernel,
        out_shape=[jax.ShapeDtypeStruct((m, N_MAIN), F32), jax.ShapeDtypeStruct((m, N_SMALL), F32)],
        grid=(m // tm, N_MAIN // tn),
        in_specs=[
            pl.BlockSpec((tm, D_MODEL), lambda i, j: (i, 0)),
            pl.BlockSpec((1, D_MODEL), lambda i, j: (0, 0)),
            _mod_spec(0, mod_row),
            _mod_spec(1, mod_row),
            pl.BlockSpec((D_MODEL, tn), lambda i, j: (0, j)),
            pl.BlockSpec((D_MODEL, N_SMALL), lambda i, j: (0, 0)),
        ],
        out_specs=[
            pl.BlockSpec((tm, tn), lambda i, j: (i, j)),
            pl.BlockSpec((tm, N_SMALL), lambda i, j: (i, 0)),
        ],
        scratch_shapes=[pltpu.VMEM((tm, D_MODEL), BF16)],
        compiler_params=_params("parallel", "arbitrary"),
        name="in_proj",
    )(x, g, mod, mod, w_main, w_small)


GLA_GROUP = 256
GLA_BLOCK = 2 * GLA_CHUNK


def _log_sigmoid(z):
    return -(jnp.maximum(-z, 0.0) + jnp.log(1.0 + jnp.exp(-jnp.abs(z))))


def _split3(x):
    hi = x.astype(BF16)
    r1 = x - hi.astype(F32)
    mid = r1.astype(BF16)
    lo = (r1 - mid.astype(F32)).astype(BF16)
    return hi, mid, lo


def _aligned(x, m):
    return x if isinstance(x, int) else pl.multiple_of(x, m)


def _gla_kernel(qc, kc, vc, rc, sc, ql, kl, vl, rl, sl, wf, wb, bf, bb, gout,
                oc_ref, ol_ref, gf_s, gb_s, oacc, st_f, st_b):
    n_ctx, n_lat = qc.shape[0], ql.shape[0]

    r = lax.broadcasted_iota(jnp.int32, (GLA_GROUP, GLA_GROUP), 0)
    c = lax.broadcasted_iota(jnp.int32, (GLA_GROUP, GLA_GROUP), 1)
    shift = GLA_BLOCK.bit_length() - 1
    same = jnp.right_shift(r, shift) == jnp.right_shift(c, shift)
    tril = jnp.where(same & (r >= c), 1.0, 0.0).astype(BF16)
    triu = jnp.where(same & (r <= c), 1.0, 0.0).astype(BF16)

    def block_cumsum(tri, la):
        return functools.reduce(jnp.add, [_dot(tri, t) for t in _split3(la)])

    def gates(s_blk, dst):
        sb = s_blk.astype(BF16)
        la_f = _log_sigmoid(_dot(sb, wf[...]) + bf[...]) * (1.0 / GLA_TAU)
        la_b = _log_sigmoid(_dot(sb, wb[...]) + bb[...]) * (1.0 / GLA_TAU)
        gf_s[pl.ds(dst, GLA_GROUP), :] = block_cumsum(tril, la_f)
        gb_s[pl.ds(dst, GLA_GROUP), :] = block_cumsum(triu, la_b)

    for i in range(n_ctx // GLA_GROUP):
        gates(sc[pl.ds(i * GLA_GROUP, GLA_GROUP), :], i * GLA_GROUP)

    def lat_gates(i, carry):
        r0 = pl.multiple_of(i * GLA_GROUP, GLA_GROUP)
        gates(sl[pl.ds(r0, GLA_GROUP), :], n_ctx + r0)
        return carry

    lax.fori_loop(0, n_lat // GLA_GROUP, lat_gates, 0)

    cr = lax.broadcasted_iota(jnp.int32, (GLA_BLOCK, GLA_BLOCK), 0)
    cc = lax.broadcasted_iota(jnp.int32, (GLA_BLOCK, GLA_BLOCK), 1)
    half = GLA_BLOCK // 2

    def scan_block(refs, r0, g0, fwd):
        q_ref, k_ref, v_ref = refs[:3]
        g_s, st = (gf_s, st_f) if fwd else (gb_s, st_b)
        rows = pl.ds(r0, GLA_BLOCK)
        q = q_ref[rows, :] * (GLA_DK ** -0.5)
        k = k_ref[rows, :]
        v = v_ref[rows, :].astype(BF16)
        g = g_s[pl.ds(g0, GLA_BLOCK), :]
        g_mid = g[half - 1:half, :] if fwd else g[half:half + 1, :]
        g_last = g[GLA_BLOCK - 1:GLA_BLOCK, :] if fwd else g[0:1, :]
        q_in = (q * jnp.exp(g - g_mid)).astype(BF16)
        k_in = (k * jnp.exp(g_mid - g)).astype(BF16)
        q_st = (q * jnp.exp(g)).astype(BF16)
        k_end = (k * jnp.exp(g_last - g)).astype(BF16)
        att = lax.dot_general(q_in, k_in, NT, preferred_element_type=F32)
        att = jnp.where((cr >= cc) if fwd else (cr <= cc), att, 0.0).astype(BF16)
        s_prev = st[...]
        o = _dot(att, v) + lax.dot_general(q_st, s_prev.astype(BF16), NT, preferred_element_type=F32)
        st[...] = s_prev * jnp.exp(g_last) + lax.dot_general(v, k_end, TN, preferred_element_type=F32)
        return o

    def step(refs, base, n_blk, i, second_half):
        r_ref, o_ref = refs[3], refs[4]
        for fwd in (True, False):
            r0 = _aligned((i if fwd else n_blk - 1 - i) * GLA_BLOCK, GLA_BLOCK)
            g0 = _aligned(base + r0, GLA_BLOCK)
            o = scan_block(refs, r0, g0, fwd)
            acc_rows = pl.ds(g0, GLA_BLOCK)
            if second_half:
                rows = pl.ds(r0, GLA_BLOCK)
                y = _rms(oacc[acc_rows, :] + o, gout[...])
                o_ref[rows, :] = (y * _silu(r_ref[rows, :])).astype(o_ref.dtype)
            else:
                oacc[acc_rows, :] = o

    st_f[...] = jnp.zeros_like(st_f)
    st_b[...] = jnp.zeros_like(st_b)
    for refs, n, base in (((qc, kc, vc, rc, oc_ref), n_ctx, 0), ((ql, kl, vl, rl, ol_ref), n_lat, n_ctx)):
        n_blk = n // GLA_BLOCK
        assert n_blk % 2 == 0
        for second_half in (False, True):
            lo = n_blk // 2 if second_half else 0
            if n_blk == 2:
                step(refs, base, n_blk, lo, second_half)
            else:
                def body(i, carry, refs=refs, base=base, n_blk=n_blk, second_half=second_half):
                    step(refs, base, n_blk, i, second_half)
                    return carry

                lax.fori_loop(lo, lo + n_blk // 2, body, 0)


def _gla(p_ctx, s_ctx, p_lat, s_lat, wf, wb, bf, bb, gout):
    def stream(n):
        return [
            pl.BlockSpec((n, GLA_DK), lambda b, h: (b, COL_Q // GLA_DK + h)),
            pl.BlockSpec((n, GLA_DK), lambda b, h: (b, COL_K // GLA_DK + h)),
            pl.BlockSpec((n, GLA_DV), lambda b, h: (b, COL_V // GLA_DV + h)),
            pl.BlockSpec((n, GLA_DV), lambda b, h: (b, COL_R // GLA_DV + h)),
            pl.BlockSpec((n, LANES), lambda b, h: (b, 0)),
        ]

    n_all = CTX_LEN + SEQ
    return pl.pallas_call(
        _gla_kernel,
        out_shape=[jax.ShapeDtypeStruct((BATCH * CTX_LEN, GLA_VW), BF16),
                   jax.ShapeDtypeStruct((BATCH * SEQ, GLA_VW), BF16)],
        grid=(BATCH, GLA_HEADS),
        in_specs=stream(CTX_LEN) + stream(SEQ) + [
            pl.BlockSpec((LANES, GLA_DK), lambda b, h: (0, h)),
            pl.BlockSpec((LANES, GLA_DK), lambda b, h: (0, h)),
            pl.BlockSpec((1, GLA_DK), lambda b, h: (0, h)),
            pl.BlockSpec((1, GLA_DK), lambda b, h: (0, h)),
            pl.BlockSpec((1, GLA_DV), lambda b, h: (0, 0)),
        ],
        out_specs=[pl.BlockSpec((CTX_LEN, GLA_DV), lambda b, h: (b, h)),
                   pl.BlockSpec((SEQ, GLA_DV), lambda b, h: (b, h))],
        scratch_shapes=[
            pltpu.VMEM((n_all, GLA_DK), F32),
            pltpu.VMEM((n_all, GLA_DK), F32),
            pltpu.VMEM((n_all, GLA_DV), F32),
            pltpu.VMEM((GLA_DV, GLA_DK), F32),
            pltpu.VMEM((GLA_DV, GLA_DK), F32),
        ],
        compiler_params=_params("parallel", "parallel"),
        name="gla_bidir",
    )(p_ctx, p_ctx, p_ctx, p_ctx, s_ctx, p_lat, p_lat, p_lat, p_lat, s_lat, wf, wb, bf, bb, gout)


def _rope(x, cos, sin):
    return x * cos + pltpu.roll(x, MLA_ROPE // 2, 1) * sin


def _mla_up_kernel(cq_ref, ckv_ref, ps_ref, cos_ref, sin_ref, gq_ref, gkv_ref, wq_ref, wkv_ref,
                   q_out, k_out, vt_out):
    cqn = _rms(cq_ref[...], gq_ref[...]).astype(BF16)
    ckvn = _rms(ckv_ref[...], gkv_ref[...]).astype(BF16)
    cos, sin = cos_ref[...], sin_ref[...]
    k_rope = _rope(ps_ref[...], cos, sin).astype(BF16)
    for h in range(MLA_HEADS):
        cols = slice(h * MLA_HEAD_W, (h + 1) * MLA_HEAD_W)
        qh = _dot(cqn, wq_ref[:, cols])
        q_out[h, :, 0:MLA_NOPE] = (qh[:, :MLA_NOPE] * SCORE_SCALE_LOG2).astype(BF16)
        q_out[h, :, MLA_NOPE:] = (_rope(qh[:, MLA_NOPE:], cos, sin) * SCORE_SCALE_LOG2).astype(BF16)
        kvh = _dot(ckvn, wkv_ref[:, cols])
        k_out[h, :, 0:MLA_NOPE] = kvh[:, :MLA_NOPE].astype(BF16)
        k_out[h, :, MLA_NOPE:] = k_rope
        vt_out[h] = kvh[:, MLA_NOPE:].T.astype(BF16)


def _mla_up(p, ps, cos, sin, gq, gkv, wq, wkv, n_pos, tm):
    nt = n_pos // tm
    row = lambda b, i: b * nt + i
    head_spec = lambda w: pl.BlockSpec((None, MLA_HEADS, tm, w), lambda b, i: (b, 0, i, 0))
    return pl.pallas_call(
        _mla_up_kernel,
        out_shape=[jax.ShapeDtypeStruct((BATCH, MLA_HEADS, n_pos, MLA_HEAD_W), BF16),
                   jax.ShapeDtypeStruct((BATCH, MLA_HEADS, n_pos, MLA_HEAD_W), BF16),
                   jax.ShapeDtypeStruct((BATCH, MLA_HEADS, MLA_V, n_pos), BF16)],
        grid=(BATCH, nt),
        in_specs=[
            pl.BlockSpec((tm, MLA_Q_RANK), lambda b, i: (row(b, i), COL_CQ // MLA_Q_RANK)),
            pl.BlockSpec((tm, MLA_KV_RANK), lambda b, i: (row(b, i), COL_CKV // MLA_KV_RANK)),
            pl.BlockSpec((tm, LANES), lambda b, i: (row(b, i), 1)),
            pl.BlockSpec((tm, LANES), lambda b, i: (i, 0)),
            pl.BlockSpec((tm, LANES), lambda b, i: (i, 0)),
            pl.BlockSpec((1, MLA_Q_RANK), lambda b, i: (0, 0)),
            pl.BlockSpec((1, MLA_KV_RANK), lambda b, i: (0, 0)),
            pl.BlockSpec((MLA_Q_RANK, MLA_HEADS * MLA_HEAD_W), lambda b, i: (0, 0)),
            pl.BlockSpec((MLA_KV_RANK, MLA_HEADS * MLA_HEAD_W), lambda b, i: (0, 0)),
        ],
        out_specs=[head_spec(MLA_HEAD_W), head_spec(MLA_HEAD_W),
                   pl.BlockSpec((None, MLA_HEADS, MLA_V, tm), lambda b, i: (b, 0, 0, i))],
        compiler_params=_params("parallel", "parallel"),
        name="mla_up",
    )(p, p, ps, cos, sin, gq, gkv, wq, wkv)


ATTN_HEADS_PER_STEP = 4


def _attn_kernel(q_ref, *refs, n_kv):
    k_refs, vt_refs, o_ref = refs[:n_kv], refs[n_kv:2 * n_kv], refs[2 * n_kv]
    for h in range(ATTN_HEADS_PER_STEP):
        q = q_ref[h]
        s = [lax.dot_general(k[h], q, NT, preferred_element_type=F32) for k in k_refs]
        m = functools.reduce(jnp.maximum, [jnp.max(x, axis=0, keepdims=True) for x in s])
        p = [jnp.exp2(x - m) for x in s]
        denom = functools.reduce(jnp.add, [jnp.sum(x, axis=0, keepdims=True) for x in p])
        o = functools.reduce(jnp.add, [_dot(vt[h], x.astype(BF16)) for x, vt in zip(p, vt_refs)])
        o_ref[:, h * MLA_V:(h + 1) * MLA_V] = (o / denom).T.astype(o_ref.dtype)


def _attention(q, ks, vs, tq):
    nq = q.shape[2]
    nt = nq // tq
    hb = ATTN_HEADS_PER_STEP
    kv_spec = lambda a: pl.BlockSpec((None, hb) + a.shape[2:], lambda b, g, i: (b, g, 0, 0))
    return pl.pallas_call(
        functools.partial(_attn_kernel, n_kv=len(ks)),
        out_shape=jax.ShapeDtypeStruct((BATCH * nq, MLA_HEADS * MLA_V), BF16),
        grid=(BATCH, MLA_HEADS // hb, nt),
        in_specs=[pl.BlockSpec((None, hb, tq, MLA_HEAD_W), lambda b, g, i: (b, g, i, 0))]
        + [kv_spec(a) for a in ks] + [kv_spec(a) for a in vs],
        out_specs=pl.BlockSpec((tq, hb * MLA_V), lambda b, g, i: (b * nt + i, g)),
        compiler_params=_params("parallel", "parallel", "parallel"),
        name="mla_attn",
    )(q, *ks, *vs)


def _merge_kernel(gla_ref, mla_ref, ga_ref, gb_ref, wa_ref, wb_ref, y_ref):
    y = (jax.nn.sigmoid(ga_ref[...]) * _dot(gla_ref[...], wa_ref[...])
         + jax.nn.sigmoid(gb_ref[...]) * _dot(mla_ref[...], wb_ref[...]))
    y_ref[...] = y.astype(y_ref.dtype)


def _merge(gla, mla, p, wa, wb, tm, tn):
    m = gla.shape[0]
    return pl.pallas_call(
        _merge_kernel,
        out_shape=jax.ShapeDtypeStruct((m, D_MODEL), BF16),
        grid=(m // tm, D_MODEL // tn),
        in_specs=[
            pl.BlockSpec((tm, GLA_VW), lambda i, j: (i, 0)),
            pl.BlockSpec((tm, MLA_HEADS * MLA_V), lambda i, j: (i, 0)),
            pl.BlockSpec((tm, tn), lambda i, j: (i, COL_GA // tn + j)),
            pl.BlockSpec((tm, tn), lambda i, j: (i, COL_GB // tn + j)),
            pl.BlockSpec((GLA_VW, tn), lambda i, j: (0, j)),
            pl.BlockSpec((MLA_HEADS * MLA_V, tn), lambda i, j: (0, j)),
        ],
        out_specs=pl.BlockSpec((tm, tn), lambda i, j: (i, j)),
        compiler_params=_params("parallel", "parallel"),
        name="branch_merge",
    )(gla, mla, p, p, wa, wb)


def _proj_residual_kernel(y_ref, w_ref, x_ref, gate_ref, o_ref):
    o_ref[...] = x_ref[...] + gate_ref[...] * _dot(y_ref[...], w_ref[...])


def _proj_residual(y, w, x, mod, which, mod_row, tm, tn):
    m, kdim = y.shape
    return pl.pallas_call(
        _proj_residual_kernel,
        out_shape=jax.ShapeDtypeStruct((m, D_MODEL), F32),
        grid=(m // tm, D_MODEL // tn),
        in_specs=[
            pl.BlockSpec((tm, kdim), lambda i, j: (i, 0)),
            pl.BlockSpec((kdim, tn), lambda i, j: (0, j)),
            pl.BlockSpec((tm, tn), lambda i, j: (i, j)),
            _mod_spec(which, mod_row, tn, col=True),
        ],
        out_specs=pl.BlockSpec((tm, tn), lambda i, j: (i, j)),
        compiler_params=_params("parallel", "parallel"),
        name="proj_residual",
    )(y, w, x, mod)


def _ffn_up_kernel(x_ref, g_ref, sh_ref, sc_ref, wg_ref, wu_ref, o_ref, h_ref):
    @pl.when(pl.program_id(1) == 0)
    def _():
        h = _rms(x_ref[...], g_ref[...]) * (1.0 + sc_ref[...]) + sh_ref[...]
        h_ref[...] = h.astype(BF16)

    h = h_ref[...]
    o_ref[...] = (_silu(_dot(h, wg_ref[...])) * _dot(h, wu_ref[...])).astype(o_ref.dtype)


def _ffn_up(x, g, mod, mod_row, wg, wu, tm, tn):
    m = x.shape[0]
    return pl.pallas_call(
        _ffn_up_kernel,
        out_shape=jax.ShapeDtypeStruct((m, D_FF), BF16),
        grid=(m // tm, D_FF // tn),
        in_specs=[
            pl.BlockSpec((tm, D_MODEL), lambda i, j: (i, 0)),
            pl.BlockSpec((1, D_MODEL), lambda i, j: (0, 0)),
            _mod_spec(3, mod_row),
            _mod_spec(4, mod_row),
            pl.BlockSpec((D_MODEL, tn), lambda i, j: (0, j)),
            pl.BlockSpec((D_MODEL, tn), lambda i, j: (0, j)),
        ],
        out_specs=pl.BlockSpec((tm, tn), lambda i, j: (i, j)),
        scratch_shapes=[pltpu.VMEM((tm, D_MODEL), BF16)],
        compiler_params=_params("parallel", "arbitrary"),
        name="ffn_up",
    )(x, g, mod, mod, wg, wu)


def _final_norm_kernel(x_ref, g_ref, o_ref):
    o_ref[...] = _rms(x_ref[...], g_ref[...])


def _final_norm(x, g, tm):
    m = x.shape[0]
    return pl.pallas_call(
        _final_norm_kernel,
        out_shape=jax.ShapeDtypeStruct((m, D_MODEL), F32),
        grid=(m // tm,),
        in_specs=[pl.BlockSpec((tm, D_MODEL), lambda i: (i, 0)),
                  pl.BlockSpec((1, D_MODEL), lambda i: (0, 0))],
        out_specs=pl.BlockSpec((tm, D_MODEL), lambda i: (i, 0)),
        compiler_params=_params("parallel"),
        name="final_norm",
    )(x, g)


def _deinterleave(w):
    x0, x1 = w[..., 0::2], w[..., 1::2]
    return jnp.concatenate([x0, x1, x0, x1], axis=-1)


def _rope_tables():
    rows = SEQ // GRID_W
    row = jnp.repeat(jnp.arange(rows, dtype=F32), GRID_W)
    col = jnp.tile(jnp.arange(GRID_W, dtype=F32), rows)
    n_pairs = MLA_ROPE // 4
    freqs = ROPE_BASE ** (-jnp.arange(n_pairs, dtype=F32) / n_pairs)
    ang = jnp.concatenate([row[:, None] * freqs, col[:, None] * freqs], axis=-1)
    cos, sin = jnp.cos(ang), jnp.sin(ang)
    pad = jnp.zeros((SEQ, LANES - MLA_ROPE), F32)
    cos_lat = jnp.concatenate([cos, cos, pad], axis=-1)
    sin_lat = jnp.concatenate([-sin, sin, pad], axis=-1)
    cos_ctx = jnp.concatenate([jnp.ones((CTX_LEN, MLA_ROPE), F32), jnp.zeros((CTX_LEN, LANES - MLA_ROPE), F32)], -1)
    sin_ctx = jnp.zeros((CTX_LEN, LANES), F32)
    return cos_lat, sin_lat, cos_ctx, sin_ctx


def kernel(x, c, ctx, c_ctx, w_mod, b_mod, g_attn, g_ffn, w_in, w_gla_up_f, b_gla_f, w_gla_up_b, b_gla_b, g_gla_out, g_q_lora, w_q_up, g_kv_lora, w_kv_up, w_branch_a, w_branch_b, w_out, w_ffn_gate, w_ffn_up, w_ffn_down, g_final):
    L = DEPTH
    o_gf = COL_R + GLA_VW
    o_cq = o_gf + 2 * GLA_RANK
    o_kr = o_cq + MLA_Q_RANK + MLA_KV_RANK
    o_ga = o_kr + MLA_ROPE
    w_main = jnp.concatenate(
        [w_in[:, :, :o_gf], w_in[:, :, o_ga:], w_in[:, :, o_cq:o_kr]], axis=-1).astype(BF16)
    w_small = jnp.concatenate(
        [w_in[:, :, o_gf:o_cq], jnp.zeros((L, D_MODEL, LANES - 2 * GLA_RANK), F32),
         _deinterleave(w_in[:, :, o_kr:o_ga])], axis=-1).astype(BF16)
    wq = w_q_up.reshape(L, MLA_Q_RANK, MLA_HEADS, MLA_QK)
    wq = jnp.concatenate([wq[..., :MLA_NOPE], _deinterleave(wq[..., MLA_NOPE:])], axis=-1)
    wq = wq.reshape(L, MLA_Q_RANK, MLA_HEADS * MLA_HEAD_W).astype(BF16)
    wkv = w_kv_up.astype(BF16)
    zpad = lambda n: jnp.zeros((L, n, GLA_QK), F32)
    wup_f = jnp.concatenate([w_gla_up_f, zpad(LANES - GLA_RANK)], axis=1).astype(BF16)
    wup_b = jnp.concatenate([zpad(GLA_RANK), w_gla_up_b, zpad(LANES - 2 * GLA_RANK)], axis=1).astype(BF16)
    wa, wb, wo = w_branch_a.astype(BF16), w_branch_b.astype(BF16), w_out.astype(BF16)
    wg, wu, wd = w_ffn_gate.astype(BF16), w_ffn_up.astype(BF16), w_ffn_down.astype(BF16)
    cos_lat, sin_lat, cos_ctx, sin_ctx = _rope_tables()

    cs = jnp.concatenate([c, c_ctx[None, :], jnp.zeros((8 - BATCH - 1, D_MODEL), F32)], axis=0)
    mods = _modulation(cs, w_mod, b_mod)

    xl = x.reshape(BATCH * SEQ, D_MODEL)
    xc = ctx.reshape(BATCH * CTX_LEN, D_MODEL)
    tm = 1024
    lat_row = lambda i: i // (SEQ // tm)
    ctx_row = lambda i: BATCH
    row2 = lambda v: v.reshape(1, -1)

    for l in range(L):
        last = l == L - 1
        mod = mods[l].reshape(8, 6, 1, D_MODEL)
        p_lat, s_lat = _inproj(xl, row2(g_attn[l]), mod, lat_row, w_main[l], w_small[l], tm, 1024)
        p_ctx, s_ctx = _inproj(xc, row2(g_attn[l]), mod, ctx_row, w_main[l], w_small[l], tm, 1024)

        gla_ctx, gla_lat = _gla(p_ctx, s_ctx, p_lat, s_lat, wup_f[l], wup_b[l],
                                row2(b_gla_f[l]), row2(b_gla_b[l]), row2(g_gla_out[l]))

        gq, gkv = row2(g_q_lora[l]), row2(g_kv_lora[l])
        q_lat, k_lat, v_lat = _mla_up(p_lat, s_lat, cos_lat, sin_lat, gq, gkv, wq[l], wkv[l], SEQ, 512)
        q_ctx, k_ctx, v_ctx = _mla_up(p_ctx, s_ctx, cos_ctx, sin_ctx, gq, gkv, wq[l], wkv[l], CTX_LEN, CTX_LEN)
        mla_lat = _attention(q_lat, [k_ctx, k_lat], [v_ctx, v_lat], 512)

        y = _merge(gla_lat, mla_lat, p_lat, wa[l], wb[l], tm, 512)
        xl = _proj_residual(y, wo[l], xl, mod, 2, lat_row, tm, 512)
        if not last:
            mla_ctx = _attention(q_ctx, [k_ctx], [v_ctx], CTX_LEN)
            y = _merge(gla_ctx, mla_ctx, p_ctx, wa[l], wb[l], tm, 512)
            xc = _proj_residual(y, wo[l], xc, mod, 2, ctx_row, tm, 512)

        hid = _ffn_up(xl, row2(g_ffn[l]), mod, lat_row, wg[l], wu[l], tm, 512)
        xl = _proj_residual(hid, wd[l], xl, mod, 5, lat_row, tm, 512)
        if not last:
            hid = _ffn_up(xc, row2(g_ffn[l]), mod, ctx_row, wg[l], wu[l], tm, 512)
            xc = _proj_residual(hid, wd[l], xc, mod, 5, ctx_row, tm, 512)

    return _final_norm(xl, row2(g_final), 512).reshape(BATCH, SEQ, D_MODEL)
```

```python
import functools

import jax
import jax.numpy as jnp
from jax import lax
from jax.experimental import pallas as pl
from jax.experimental.pallas import tpu as pltpu

F32 = jnp.float32
BF16 = jnp.bfloat16

D_MODEL = 2048
BATCH = 4
SEQ = 2048
DEPTH = 2
CTX_LEN = 256
GRID_W = 64
EPS = 1e-6

GLA_HEADS = 4
GLA_DK = 256
GLA_DV = 512
GLA_QK = GLA_HEADS * GLA_DK
GLA_VW = GLA_HEADS * GLA_DV
GLA_RANK = 16
GLA_TAU = 16.0
GLA_CHUNK = 64

MLA_HEADS = 16
MLA_Q_RANK = 512
MLA_KV_RANK = 512
MLA_NOPE = 128
MLA_ROPE = 64
MLA_V = 128
MLA_QK = MLA_NOPE + MLA_ROPE
MLA_VW = MLA_HEADS * MLA_V
ROPE_BASE = 10000.0
D_FF = 5632

SRC_GATE = 2 * GLA_QK + 2 * GLA_VW
SRC_CQ = SRC_GATE + 2 * GLA_RANK
SRC_KR = SRC_CQ + MLA_Q_RANK + MLA_KV_RANK
SRC_GA = SRC_KR + MLA_ROPE
COL_Q = 0
COL_K = COL_Q + GLA_QK
COL_V = COL_K + GLA_QK
COL_R = COL_V + GLA_VW
COL_GA = COL_R + GLA_VW
COL_GB = COL_GA + D_MODEL
COL_CQ = COL_GB + D_MODEL
COL_CKV = COL_CQ + MLA_Q_RANK
N_MAIN = COL_CKV + MLA_KV_RANK
N_SMALL = 256
LANES = 128
MLA_HEAD_W = 256
SCORE_SCALE_LOG2 = (MLA_QK ** -0.5) * 1.4426950408889634

V7X_VMEM_LIMIT = 56 * 1024 * 1024

NT = (((1,), (1,)), ((), ()))
TN = (((0,), (0,)), ((), ()))


def _params(*sem):
    return pltpu.CompilerParams(dimension_semantics=sem, vmem_limit_bytes=V7X_VMEM_LIMIT)


def _dot(a, b):
    return jnp.dot(a, b, preferred_element_type=F32)


def _rms(x, g):
    return x * lax.rsqrt(jnp.mean(x * x, axis=-1, keepdims=True) + EPS) * g


def _silu(x):
    return x * jax.nn.sigmoid(x)


def _layer_spec(l, block, index):
    return pl.BlockSpec((None,) + block, lambda *g: (l,) + index(*g))


def _layer_vec(l, n):
    return _layer_spec(l, (1, n), lambda *g: (0, 0))


def _mod_spec(l, which, mod_row, tn=D_MODEL, col=False):
    if col:
        return pl.BlockSpec((None, None, None, 1, tn), lambda i, j: (l, mod_row(i), which, 0, j))
    return pl.BlockSpec((None, None, None, 1, tn), lambda i, j: (l, mod_row(i), which, 0, 0))


def _mod_kernel(cs_ref, w_ref, b_ref, o_ref):
    h = _silu(cs_ref[...]).astype(BF16)
    o_ref[...] = _dot(h, w_ref[...].astype(BF16)) + b_ref[...]


def _modulation(cs, w_mod, b_mod):
    tn = D_MODEL
    return pl.pallas_call(
        _mod_kernel,
        out_shape=jax.ShapeDtypeStruct((DEPTH, 8, 6 * D_MODEL), F32),
        grid=(DEPTH, 6 * D_MODEL // tn),
        in_specs=[
            pl.BlockSpec((8, D_MODEL), lambda l, j: (0, 0)),
            pl.BlockSpec((None, D_MODEL, tn), lambda l, j: (l, 0, j)),
            pl.BlockSpec((None, 1, tn), lambda l, j: (l, 0, j)),
        ],
        out_specs=pl.BlockSpec((None, 8, tn), lambda l, j: (l, 0, j)),
        compiler_params=_params("parallel", "parallel"),
        name="adaln_mod",
    )(cs, w_mod, b_mod.reshape(DEPTH, 1, 6 * D_MODEL))


PREP_TN = 512
PREP_PLAIN = COL_GA // PREP_TN
PREP_GATES = (COL_CQ - COL_GA) // PREP_TN
PREP_TILES = N_MAIN // PREP_TN


def _w_main_kernel(a_ref, b_ref, o_ref):
    j = pl.program_id(1)

    def shifted(shift):
        cat = jnp.concatenate([a_ref[...], b_ref[...]], axis=1)
        o_ref[...] = cat[:, shift:shift + PREP_TN].astype(BF16)

    @pl.when(j < PREP_PLAIN)
    def _():
        o_ref[...] = a_ref[...].astype(BF16)

    @pl.when((j >= PREP_PLAIN) & (j < PREP_PLAIN + PREP_GATES))
    def _():
        shifted(SRC_GA % PREP_TN)

    @pl.when(j >= PREP_PLAIN + PREP_GATES)
    def _():
        shifted(SRC_CQ % PREP_TN)


def _w_main(w_in):
    def src_tile(j):
        gates = SRC_GA // PREP_TN + (j - PREP_PLAIN)
        lora = SRC_CQ // PREP_TN + (j - PREP_PLAIN - PREP_GATES)
        return jnp.where(j < PREP_PLAIN, j, jnp.where(j < PREP_PLAIN + PREP_GATES, gates, lora))

    nxt_tile = lambda j: jnp.where(j < PREP_PLAIN, SRC_GA // PREP_TN, src_tile(j)) + 1
    return pl.pallas_call(
        _w_main_kernel,
        out_shape=jax.ShapeDtypeStruct((DEPTH, D_MODEL, N_MAIN), BF16),
        grid=(DEPTH, PREP_TILES),
        in_specs=[pl.BlockSpec((None, D_MODEL, PREP_TN), lambda l, j: (l, 0, src_tile(j))),
                  pl.BlockSpec((None, D_MODEL, PREP_TN), lambda l, j: (l, 0, nxt_tile(j)))],
        out_specs=pl.BlockSpec((None, D_MODEL, PREP_TN), lambda l, j: (l, 0, j)),
        compiler_params=_params("parallel", "arbitrary"),
        name="w_main_layout",
    )(w_in, w_in)


def _inproj_kernel(x_ref, g_ref, sh_ref, sc_ref, w_ref, ws_ref, p_ref, ps_ref, h_ref):
    @pl.when(pl.program_id(1) == 0)
    def _():
        h = _rms(x_ref[...], g_ref[...]) * (1.0 + sc_ref[...]) + sh_ref[...]
        h = h.astype(BF16)
        h_ref[...] = h
        ps_ref[...] = _dot(h, ws_ref[...])

    p_ref[...] = _dot(h_ref[...], w_ref[...]).astype(p_ref.dtype)


def _inproj(l, x, g, mods, mod_row, w_main, w_small, tm, tn):
    m = x.shape[0]
    return pl.pallas_call(
        _inproj_kernel,
        out_shape=[jax.ShapeDtypeStruct((m, N_MAIN), BF16), jax.ShapeDtypeStruct((m, N_SMALL), F32)],
        grid=(m // tm, N_MAIN // tn),
        in_specs=[
            pl.BlockSpec((tm, D_MODEL), lambda i, j: (i, 0)),
            _layer_vec(l, D_MODEL),
            _mod_spec(l, 0, mod_row),
            _mod_spec(l, 1, mod_row),
            _layer_spec(l, (D_MODEL, tn), lambda i, j: (0, j)),
            _layer_spec(l, (D_MODEL, N_SMALL), lambda i, j: (0, 0)),
        ],
        out_specs=[
            pl.BlockSpec((tm, tn), lambda i, j: (i, j)),
            pl.BlockSpec((tm, N_SMALL), lambda i, j: (i, 0)),
        ],
        scratch_shapes=[pltpu.VMEM((tm, D_MODEL), BF16)],
        compiler_params=_params("parallel", "arbitrary"),
        name="in_proj",
    )(x, g, mods, mods, w_main, w_small)


GLA_GROUP = 256
GLA_BLOCK = 2 * GLA_CHUNK


def _log_sigmoid(z):
    return -(jnp.maximum(-z, 0.0) + jnp.log(1.0 + jnp.exp(-jnp.abs(z))))


def _split3(x):
    hi = x.astype(BF16)
    r1 = x - hi.astype(F32)
    mid = r1.astype(BF16)
    lo = (r1 - mid.astype(F32)).astype(BF16)
    return hi, mid, lo


def _aligned(x, m):
    return x if isinstance(x, int) else pl.multiple_of(x, m)


def _gla_kernel(qc, kc, vc, rc, sc, ql, kl, vl, rl, sl, wf, wb, bf, bb, gout,
                oc_ref, ol_ref, gf_s, gb_s, oacc, st_f, st_b):
    n_ctx, n_lat = qc.shape[0], ql.shape[0]

    r = lax.broadcasted_iota(jnp.int32, (GLA_GROUP, GLA_GROUP), 0)
    c = lax.broadcasted_iota(jnp.int32, (GLA_GROUP, GLA_GROUP), 1)
    shift = GLA_BLOCK.bit_length() - 1
    same = jnp.right_shift(r, shift) == jnp.right_shift(c, shift)
    tril = jnp.where(same & (r >= c), 1.0, 0.0).astype(BF16)
    triu = jnp.where(same & (r <= c), 1.0, 0.0).astype(BF16)

    def block_cumsum(tri, la):
        return functools.reduce(jnp.add, [_dot(tri, t) for t in _split3(la)])

    def gates(s_blk, dst):
        sb = s_blk.astype(BF16)
        la_f = _log_sigmoid(_dot(sb, wf[...]) + bf[...]) * (1.0 / GLA_TAU)
        la_b = _log_sigmoid(_dot(sb, wb[...]) + bb[...]) * (1.0 / GLA_TAU)
        gf_s[pl.ds(dst, GLA_GROUP), :] = block_cumsum(tril, la_f)
        gb_s[pl.ds(dst, GLA_GROUP), :] = block_cumsum(triu, la_b)

    for i in range(n_ctx // GLA_GROUP):
        gates(sc[pl.ds(i * GLA_GROUP, GLA_GROUP), :], i * GLA_GROUP)

    def lat_gates(i, carry):
        r0 = pl.multiple_of(i * GLA_GROUP, GLA_GROUP)
        gates(sl[pl.ds(r0, GLA_GROUP), :], n_ctx + r0)
        return carry

    lax.fori_loop(0, n_lat // GLA_GROUP, lat_gates, 0)

    cr = lax.broadcasted_iota(jnp.int32, (GLA_BLOCK, GLA_BLOCK), 0)
    cc = lax.broadcasted_iota(jnp.int32, (GLA_BLOCK, GLA_BLOCK), 1)
    half = GLA_BLOCK // 2

    def scan_block(refs, r0, g0, fwd):
        q_ref, k_ref, v_ref = refs[:3]
        g_s, st = (gf_s, st_f) if fwd else (gb_s, st_b)
        rows = pl.ds(r0, GLA_BLOCK)
        q = q_ref[rows, :].astype(F32) * (GLA_DK ** -0.5)
        k = k_ref[rows, :].astype(F32)
        v = v_ref[rows, :]
        g = g_s[pl.ds(g0, GLA_BLOCK), :]
        g_mid = g[half - 1:half, :] if fwd else g[half:half + 1, :]
        g_last = g[GLA_BLOCK - 1:GLA_BLOCK, :] if fwd else g[0:1, :]
        q_in = (q * jnp.exp(g - g_mid)).astype(BF16)
        k_in = (k * jnp.exp(g_mid - g)).astype(BF16)
        q_st = (q * jnp.exp(g)).astype(BF16)
        k_end = (k * jnp.exp(g_last - g)).astype(BF16)
        att = lax.dot_general(q_in, k_in, NT, preferred_element_type=F32)
        att = jnp.where((cr >= cc) if fwd else (cr <= cc), att, 0.0).astype(BF16)
        s_prev = st[...]
        o = _dot(att, v) + lax.dot_general(q_st, s_prev.astype(BF16), NT, preferred_element_type=F32)
        st[...] = s_prev * jnp.exp(g_last) + lax.dot_general(v, k_end, TN, preferred_element_type=F32)
        return o

    def step(refs, base, n_blk, i, second_half):
        r_ref, o_ref = refs[3], refs[4]
        for fwd in (True, False):
            r0 = _aligned((i if fwd else n_blk - 1 - i) * GLA_BLOCK, GLA_BLOCK)
            g0 = _aligned(base + r0, GLA_BLOCK)
            o = scan_block(refs, r0, g0, fwd)
            acc_rows = pl.ds(g0, GLA_BLOCK)
            if second_half:
                rows = pl.ds(r0, GLA_BLOCK)
                y = _rms(oacc[acc_rows, :] + o, gout[...])
                o_ref[rows, :] = (y * _silu(r_ref[rows, :].astype(F32))).astype(o_ref.dtype)
            else:
                oacc[acc_rows, :] = o

    st_f[...] = jnp.zeros_like(st_f)
    st_b[...] = jnp.zeros_like(st_b)
    for refs, n, base in (((qc, kc, vc, rc, oc_ref), n_ctx, 0), ((ql, kl, vl, rl, ol_ref), n_lat, n_ctx)):
        n_blk = n // GLA_BLOCK
        assert n_blk % 2 == 0
        for second_half in (False, True):
            lo = n_blk // 2 if second_half else 0
            if n_blk == 2:
                step(refs, base, n_blk, lo, second_half)
            else:
                def body(i, carry, refs=refs, base=base, n_blk=n_blk, second_half=second_half):
                    step(refs, base, n_blk, i, second_half)
                    return carry

                lax.fori_loop(lo, lo + n_blk // 2, body, 0)


def _gla(l, p_ctx, s_ctx, p_lat, s_lat, wf, wb, bf, bb, gout):
    def stream(n):
        return [
            pl.BlockSpec((n, GLA_DK), lambda b, h: (b, COL_Q // GLA_DK + h)),
            pl.BlockSpec((n, GLA_DK), lambda b, h: (b, COL_K // GLA_DK + h)),
            pl.BlockSpec((n, GLA_DV), lambda b, h: (b, COL_V // GLA_DV + h)),
            pl.BlockSpec((n, GLA_DV), lambda b, h: (b, COL_R // GLA_DV + h)),
            pl.BlockSpec((n, LANES), lambda b, h: (b, 0)),
        ]

    n_all = CTX_LEN + SEQ
    return pl.pallas_call(
        _gla_kernel,
        out_shape=[jax.ShapeDtypeStruct((BATCH * CTX_LEN, GLA_VW), BF16),
                   jax.ShapeDtypeStruct((BATCH * SEQ, GLA_VW), BF16)],
        grid=(BATCH, GLA_HEADS),
        in_specs=stream(CTX_LEN) + stream(SEQ) + [
            _layer_spec(l, (LANES, GLA_DK), lambda b, h: (0, h)),
            _layer_spec(l, (LANES, GLA_DK), lambda b, h: (0, h)),
            _layer_spec(l, (1, GLA_DK), lambda b, h: (0, h)),
            _layer_spec(l, (1, GLA_DK), lambda b, h: (0, h)),
            _layer_vec(l, GLA_DV),
        ],
        out_specs=[pl.BlockSpec((CTX_LEN, GLA_DV), lambda b, h: (b, h)),
                   pl.BlockSpec((SEQ, GLA_DV), lambda b, h: (b, h))],
        scratch_shapes=[
            pltpu.VMEM((n_all, GLA_DK), F32),
            pltpu.VMEM((n_all, GLA_DK), F32),
            pltpu.VMEM((n_all, GLA_DV), F32),
            pltpu.VMEM((GLA_DV, GLA_DK), F32),
            pltpu.VMEM((GLA_DV, GLA_DK), F32),
        ],
        compiler_params=_params("parallel", "parallel"),
        name="gla_bidir",
    )(p_ctx, p_ctx, p_ctx, p_ctx, s_ctx, p_lat, p_lat, p_lat, p_lat, s_lat, wf, wb, bf, bb, gout)


def _rope(x, cos, sin):
    return x * cos + pltpu.roll(x, MLA_ROPE // 2, 1) * sin


def _mla_up_kernel(cq_ref, ckv_ref, ps_ref, cos_ref, sin_ref, gq_ref, gkv_ref, wq_ref, wkv_ref,
                   q_out, k_out, vt_out):
    cqn = _rms(cq_ref[...].astype(F32), gq_ref[...]).astype(BF16)
    ckvn = _rms(ckv_ref[...].astype(F32), gkv_ref[...]).astype(BF16)
    cos, sin = cos_ref[...], sin_ref[...]
    k_rope = _rope(ps_ref[...], cos, sin).astype(BF16)
    for h in range(MLA_HEADS):
        cols = slice(h * MLA_HEAD_W, (h + 1) * MLA_HEAD_W)
        qh = _dot(cqn, wq_ref[:, cols])
        q_out[h, :, 0:MLA_NOPE] = (qh[:, :MLA_NOPE] * SCORE_SCALE_LOG2).astype(BF16)
        q_out[h, :, MLA_NOPE:] = (_rope(qh[:, MLA_NOPE:], cos, sin) * SCORE_SCALE_LOG2).astype(BF16)
        kvh = _dot(ckvn, wkv_ref[:, cols])
        k_out[h, :, 0:MLA_NOPE] = kvh[:, :MLA_NOPE].astype(BF16)
        k_out[h, :, MLA_NOPE:] = k_rope
        vt_out[h] = kvh[:, MLA_NOPE:].T.astype(BF16)


def _mla_up(l, p, ps, cos, sin, gq, gkv, wq, wkv, n_pos, tm):
    nt = n_pos // tm
    row = lambda b, i: b * nt + i
    head_spec = lambda w: pl.BlockSpec((None, MLA_HEADS, tm, w), lambda b, i: (b, 0, i, 0))
    return pl.pallas_call(
        _mla_up_kernel,
        out_shape=[jax.ShapeDtypeStruct((BATCH, MLA_HEADS, n_pos, MLA_HEAD_W), BF16),
                   jax.ShapeDtypeStruct((BATCH, MLA_HEADS, n_pos, MLA_HEAD_W), BF16),
                   jax.ShapeDtypeStruct((BATCH, MLA_HEADS, MLA_V, n_pos), BF16)],
        grid=(BATCH, nt),
        in_specs=[
            pl.BlockSpec((tm, MLA_Q_RANK), lambda b, i: (row(b, i), COL_CQ // MLA_Q_RANK)),
            pl.BlockSpec((tm, MLA_KV_RANK), lambda b, i: (row(b, i), COL_CKV // MLA_KV_RANK)),
            pl.BlockSpec((tm, LANES), lambda b, i: (row(b, i), 1)),
            pl.BlockSpec((tm, LANES), lambda b, i: (i, 0)),
            pl.BlockSpec((tm, LANES), lambda b, i: (i, 0)),
            _layer_vec(l, MLA_Q_RANK),
            _layer_vec(l, MLA_KV_RANK),
            _layer_spec(l, (MLA_Q_RANK, MLA_HEADS * MLA_HEAD_W), lambda b, i: (0, 0)),
            _layer_spec(l, (MLA_KV_RANK, MLA_HEADS * MLA_HEAD_W), lambda b, i: (0, 0)),
        ],
        out_specs=[head_spec(MLA_HEAD_W), head_spec(MLA_HEAD_W),
                   pl.BlockSpec((None, MLA_HEADS, MLA_V, tm), lambda b, i: (b, 0, 0, i))],
        compiler_params=_params("parallel", "parallel"),
        name="mla_up",
    )(p, p, ps, cos, sin, gq, gkv, wq, wkv)


ATTN_HEADS_PER_STEP = 4


def _attn_kernel(q_ref, *refs, n_kv):
    k_refs, vt_refs, o_ref = refs[:n_kv], refs[n_kv:2 * n_kv], refs[2 * n_kv]
    for h in range(ATTN_HEADS_PER_STEP):
        q = q_ref[h]
        s = [lax.dot_general(k[h], q, NT, preferred_element_type=F32) for k in k_refs]
        m = functools.reduce(jnp.maximum, [jnp.max(x, axis=0, keepdims=True) for x in s])
        p = [jnp.exp2(x - m) for x in s]
        denom = functools.reduce(jnp.add, [jnp.sum(x, axis=0, keepdims=True) for x in p])
        o = functools.reduce(jnp.add, [_dot(vt[h], x.astype(BF16)) for x, vt in zip(p, vt_refs)])
        o_ref[:, h * MLA_V:(h + 1) * MLA_V] = (o / denom).T.astype(o_ref.dtype)


def _attention(q, ks, vs, tq):
    nq = q.shape[2]
    nt = nq // tq
    hb = ATTN_HEADS_PER_STEP
    kv_spec = lambda a: pl.BlockSpec((None, hb) + a.shape[2:], lambda b, g, i: (b, g, 0, 0))
    return pl.pallas_call(
        functools.partial(_attn_kernel, n_kv=len(ks)),
        out_shape=jax.ShapeDtypeStruct((BATCH * nq, MLA_VW), BF16),
        grid=(BATCH, MLA_HEADS // hb, nt),
        in_specs=[pl.BlockSpec((None, hb, tq, MLA_HEAD_W), lambda b, g, i: (b, g, i, 0))]
        + [kv_spec(a) for a in ks] + [kv_spec(a) for a in vs],
        out_specs=pl.BlockSpec((tq, hb * MLA_V), lambda b, g, i: (b * nt + i, g)),
        compiler_params=_params("parallel", "parallel", "parallel"),
        name="mla_attn",
    )(q, *ks, *vs)


def _merge_kernel(gla_ref, mla_ref, ga_ref, gb_ref, wa_ref, wb_ref, y_ref):
    y = (jax.nn.sigmoid(ga_ref[...].astype(F32)) * _dot(gla_ref[...], wa_ref[...].astype(BF16))
         + jax.nn.sigmoid(gb_ref[...].astype(F32)) * _dot(mla_ref[...], wb_ref[...].astype(BF16)))
    y_ref[...] = y.astype(y_ref.dtype)


def _merge(l, gla, mla, p, wa, wb, tm, tn):
    m = gla.shape[0]
    return pl.pallas_call(
        _merge_kernel,
        out_shape=jax.ShapeDtypeStruct((m, D_MODEL), BF16),
        grid=(m // tm, D_MODEL // tn),
        in_specs=[
            pl.BlockSpec((tm, GLA_VW), lambda i, j: (i, 0)),
            pl.BlockSpec((tm, MLA_VW), lambda i, j: (i, 0)),
            pl.BlockSpec((tm, tn), lambda i, j: (i, COL_GA // tn + j)),
            pl.BlockSpec((tm, tn), lambda i, j: (i, COL_GB // tn + j)),
            _layer_spec(l, (GLA_VW, tn), lambda i, j: (0, j)),
            _layer_spec(l, (MLA_VW, tn), lambda i, j: (0, j)),
        ],
        out_specs=pl.BlockSpec((tm, tn), lambda i, j: (i, j)),
        compiler_params=_params("parallel", "parallel"),
        name="branch_merge",
    )(gla, mla, p, p, wa, wb)


def _proj_residual_kernel(y_ref, w_ref, x_ref, gate_ref, o_ref):
    o_ref[...] = x_ref[...] + gate_ref[...] * _dot(y_ref[...], w_ref[...].astype(BF16))


def _proj_residual(l, y, w, x, mods, which, mod_row, tm, tn):
    m, kdim = y.shape
    return pl.pallas_call(
        _proj_residual_kernel,
        out_shape=jax.ShapeDtypeStruct((m, D_MODEL), F32),
        grid=(m // tm, D_MODEL // tn),
        in_specs=[
            pl.BlockSpec((tm, kdim), lambda i, j: (i, 0)),
            _layer_spec(l, (kdim, tn), lambda i, j: (0, j)),
            pl.BlockSpec((tm, tn), lambda i, j: (i, j)),
            _mod_spec(l, which, mod_row, tn, col=True),
        ],
        out_specs=pl.BlockSpec((tm, tn), lambda i, j: (i, j)),
        compiler_params=_params("parallel", "parallel"),
        name="proj_residual",
    )(y, w, x, mods)


def _ffn_up_kernel(x_ref, g_ref, sh_ref, sc_ref, wg_ref, wu_ref, o_ref, h_ref):
    @pl.when(pl.program_id(1) == 0)
    def _():
        h = _rms(x_ref[...], g_ref[...]) * (1.0 + sc_ref[...]) + sh_ref[...]
        h_ref[...] = h.astype(BF16)

    h = h_ref[...]
    gate = _dot(h, wg_ref[...].astype(BF16))
    o_ref[...] = (_silu(gate) * _dot(h, wu_ref[...].astype(BF16))).astype(o_ref.dtype)


def _ffn_up(l, x, g, mods, mod_row, wg, wu, tm, tn):
    m = x.shape[0]
    return pl.pallas_call(
        _ffn_up_kernel,
        out_shape=jax.ShapeDtypeStruct((m, D_FF), BF16),
        grid=(m // tm, D_FF // tn),
        in_specs=[
            pl.BlockSpec((tm, D_MODEL), lambda i, j: (i, 0)),
            _layer_vec(l, D_MODEL),
            _mod_spec(l, 3, mod_row),
            _mod_spec(l, 4, mod_row),
            _layer_spec(l, (D_MODEL, tn), lambda i, j: (0, j)),
            _layer_spec(l, (D_MODEL, tn), lambda i, j: (0, j)),
        ],
        out_specs=pl.BlockSpec((tm, tn), lambda i, j: (i, j)),
        scratch_shapes=[pltpu.VMEM((tm, D_MODEL), BF16)],
        compiler_params=_params("parallel", "arbitrary"),
        name="ffn_up",
    )(x, g, mods, mods, wg, wu)


def _final_norm_kernel(x_ref, g_ref, o_ref):
    o_ref[...] = _rms(x_ref[...], g_ref[...])


def _final_norm(x, g, tm):
    m = x.shape[0]
    return pl.pallas_call(
        _final_norm_kernel,
        out_shape=jax.ShapeDtypeStruct((m, D_MODEL), F32),
        grid=(m // tm,),
        in_specs=[pl.BlockSpec((tm, D_MODEL), lambda i: (i, 0)),
                  pl.BlockSpec((1, D_MODEL), lambda i: (0, 0))],
        out_specs=pl.BlockSpec((tm, D_MODEL), lambda i: (i, 0)),
        compiler_params=_params("parallel"),
        name="final_norm",
    )(x, g)


def _deinterleave(w):
    x0, x1 = w[..., 0::2], w[..., 1::2]
    return jnp.concatenate([x0, x1, x0, x1], axis=-1)


def _rope_tables():
    rows = SEQ // GRID_W
    row = jnp.repeat(jnp.arange(rows, dtype=F32), GRID_W)
    col = jnp.tile(jnp.arange(GRID_W, dtype=F32), rows)
    n_pairs = MLA_ROPE // 4
    freqs = ROPE_BASE ** (-jnp.arange(n_pairs, dtype=F32) / n_pairs)
    ang = jnp.concatenate([row[:, None] * freqs, col[:, None] * freqs], axis=-1)
    cos, sin = jnp.cos(ang), jnp.sin(ang)
    pad = jnp.zeros((SEQ, LANES - MLA_ROPE), F32)
    cos_lat = jnp.concatenate([cos, cos, pad], axis=-1)
    sin_lat = jnp.concatenate([-sin, sin, pad], axis=-1)
    cos_ctx = jnp.concatenate([jnp.ones((CTX_LEN, MLA_ROPE), F32), jnp.zeros((CTX_LEN, LANES - MLA_ROPE), F32)], -1)
    sin_ctx = jnp.zeros((CTX_LEN, LANES), F32)
    return cos_lat, sin_lat, cos_ctx, sin_ctx


def kernel(x, c, ctx, c_ctx, w_mod, b_mod, g_attn, g_ffn, w_in, w_gla_up_f, b_gla_f, w_gla_up_b, b_gla_b, g_gla_out, g_q_lora, w_q_up, g_kv_lora, w_kv_up, w_branch_a, w_branch_b, w_out, w_ffn_gate, w_ffn_up, w_ffn_down, g_final):
    L = DEPTH
    w_main = _w_main(w_in)
    w_small = jnp.concatenate(
        [w_in[:, :, SRC_GATE:SRC_CQ], jnp.zeros((L, D_MODEL, LANES - 2 * GLA_RANK), F32),
         _deinterleave(w_in[:, :, SRC_KR:SRC_GA])], axis=-1).astype(BF16)
    wq = w_q_up.reshape(L, MLA_Q_RANK, MLA_HEADS, MLA_QK)
    wq = jnp.concatenate([wq[..., :MLA_NOPE], _deinterleave(wq[..., MLA_NOPE:])], axis=-1)
    wq = wq.reshape(L, MLA_Q_RANK, MLA_HEADS * MLA_HEAD_W).astype(BF16)
    wkv = w_kv_up.astype(BF16)
    zpad = lambda n: jnp.zeros((L, n, GLA_QK), F32)
    wup_f = jnp.concatenate([w_gla_up_f, zpad(LANES - GLA_RANK)], axis=1).astype(BF16)
    wup_b = jnp.concatenate([zpad(GLA_RANK), w_gla_up_b, zpad(LANES - 2 * GLA_RANK)], axis=1).astype(BF16)
    wo = w_out.astype(BF16)
    cos_lat, sin_lat, cos_ctx, sin_ctx = _rope_tables()
    vec = lambda v: v.reshape(L, 1, -1)
    g_attn, g_ffn, g_gla_out, g_q_lora, g_kv_lora = map(vec, (g_attn, g_ffn, g_gla_out, g_q_lora, g_kv_lora))
    b_gla_f, b_gla_b = vec(b_gla_f), vec(b_gla_b)

    cs = jnp.concatenate([c, c_ctx[None, :], jnp.zeros((8 - BATCH - 1, D_MODEL), F32)], axis=0)
    mods = _modulation(cs, w_mod, b_mod).reshape(L, 8, 6, 1, D_MODEL)

    xl = x.reshape(BATCH * SEQ, D_MODEL)
    xc = ctx.reshape(BATCH * CTX_LEN, D_MODEL)
    tm = 1024
    lat_row = lambda i: i // (SEQ // tm)
    ctx_row = lambda i: BATCH

    for l in range(L):
        last = l == L - 1
        p_lat, s_lat = _inproj(l, xl, g_attn, mods, lat_row, w_main, w_small, tm, 1024)
        p_ctx, s_ctx = _inproj(l, xc, g_attn, mods, ctx_row, w_main, w_small, tm, 1024)

        gla_ctx, gla_lat = _gla(l, p_ctx, s_ctx, p_lat, s_lat, wup_f, wup_b, b_gla_f, b_gla_b, g_gla_out)

        q_lat, k_lat, v_lat = _mla_up(l, p_lat, s_lat, cos_lat, sin_lat, g_q_lora, g_kv_lora, wq, wkv, SEQ, 512)
        q_ctx, k_ctx, v_ctx = _mla_up(l, p_ctx, s_ctx, cos_ctx, sin_ctx, g_q_lora, g_kv_lora, wq, wkv,
                                      CTX_LEN, CTX_LEN)
        mla_lat = _attention(q_lat, [k_ctx, k_lat], [v_ctx, v_lat], 512)

        y = _merge(l, gla_lat, mla_lat, p_lat, w_branch_a, w_branch_b, tm, 512)
        xl = _proj_residual(l, y, wo, xl, mods, 2, lat_row, tm, 512)
        if not last:
            mla_ctx = _attention(q_ctx, [k_ctx], [v_ctx], CTX_LEN)
            y = _merge(l, gla_ctx, mla_ctx, p_ctx, w_branch_a, w_branch_b, tm, 512)
            xc = _proj_residual(l, y, wo, xc, mods, 2, ctx_row, tm, 512)

        hid = _ffn_up(l, xl, g_ffn, mods, lat_row, w_ffn_gate, w_ffn_up, tm, 512)
        xl = _proj_residual(l, hid, w_ffn_down, xl, mods, 5, lat_row, tm, 256)
        if not last:
            hid = _ffn_up(l, xc, g_ffn, mods, ctx_row, w_ffn_gate, w_ffn_up, tm, 512)
            xc = _proj_residual(l, hid, w_ffn_down, xc, mods, 5, ctx_row, tm, 256)

    return _final_norm(xl, g_final.reshape(1, -1), 512).reshape(BATCH, SEQ, D_MODEL)
```

```python
import functools

import jax
import jax.numpy as jnp
from jax import lax
from jax.experimental import pallas as pl
from jax.experimental.pallas import tpu as pltpu

F32 = jnp.float32
BF16 = jnp.bfloat16

D_MODEL = 2048
BATCH = 4
SEQ = 2048
DEPTH = 2
CTX_LEN = 256
GRID_W = 64
EPS = 1e-6

GLA_HEADS = 4
GLA_DK = 256
GLA_DV = 512
GLA_QK = GLA_HEADS * GLA_DK
GLA_VW = GLA_HEADS * GLA_DV
GLA_RANK = 16
GLA_TAU = 16.0
GLA_CHUNK = 64

MLA_HEADS = 16
MLA_Q_RANK = 512
MLA_KV_RANK = 512
MLA_NOPE = 128
MLA_ROPE = 64
MLA_V = 128
MLA_QK = MLA_NOPE + MLA_ROPE
MLA_VW = MLA_HEADS * MLA_V
ROPE_BASE = 10000.0
D_FF = 5632

SRC_GATE = 2 * GLA_QK + 2 * GLA_VW
SRC_CQ = SRC_GATE + 2 * GLA_RANK
SRC_KR = SRC_CQ + MLA_Q_RANK + MLA_KV_RANK
SRC_GA = SRC_KR + MLA_ROPE
COL_Q = 0
COL_K = COL_Q + GLA_QK
COL_V = COL_K + GLA_QK
COL_R = COL_V + GLA_VW
COL_GA = COL_R + GLA_VW
COL_GB = COL_GA + D_MODEL
COL_CQ = COL_GB + D_MODEL
COL_CKV = COL_CQ + MLA_Q_RANK
N_MAIN = COL_CKV + MLA_KV_RANK
N_SMALL = 256
LANES = 128
MLA_HEAD_W = 256
SCORE_SCALE_LOG2 = (MLA_QK ** -0.5) * 1.4426950408889634

V7X_VMEM_LIMIT = 56 * 1024 * 1024

NT = (((1,), (1,)), ((), ()))
TN = (((0,), (0,)), ((), ()))


def _params(*sem):
    return pltpu.CompilerParams(dimension_semantics=sem, vmem_limit_bytes=V7X_VMEM_LIMIT)


def _dot(a, b):
    return jnp.dot(a, b, preferred_element_type=F32)


def _rms(x, g):
    return x * lax.rsqrt(jnp.mean(x * x, axis=-1, keepdims=True) + EPS) * g


def _silu(x):
    return x * jax.nn.sigmoid(x)


def _layer_spec(l, block, index):
    return pl.BlockSpec((None,) + block, lambda *g: (l,) + index(*g))


def _layer_vec(l, n):
    return _layer_spec(l, (1, n), lambda *g: (0, 0))


def _mod_spec(l, which, mod_row, tn=D_MODEL, col=False):
    if col:
        return pl.BlockSpec((None, None, None, 1, tn), lambda i, j: (l, mod_row(i), which, 0, j))
    return pl.BlockSpec((None, None, None, 1, tn), lambda i, j: (l, mod_row(i), which, 0, 0))


def _mod_kernel(cs_ref, w_ref, b_ref, o_ref):
    h = _silu(cs_ref[...]).astype(BF16)
    o_ref[...] = _dot(h, w_ref[...].astype(BF16)) + b_ref[...]


def _modulation(cs, w_mod, b_mod):
    tn = D_MODEL
    return pl.pallas_call(
        _mod_kernel,
        out_shape=jax.ShapeDtypeStruct((DEPTH, 8, 6 * D_MODEL), F32),
        grid=(DEPTH, 6 * D_MODEL // tn),
        in_specs=[
            pl.BlockSpec((8, D_MODEL), lambda l, j: (0, 0)),
            pl.BlockSpec((None, D_MODEL, tn), lambda l, j: (l, 0, j)),
            pl.BlockSpec((None, 1, tn), lambda l, j: (l, 0, j)),
        ],
        out_specs=pl.BlockSpec((None, 8, tn), lambda l, j: (l, 0, j)),
        compiler_params=_params("parallel", "parallel"),
        name="adaln_mod",
    )(cs, w_mod, b_mod.reshape(DEPTH, 1, 6 * D_MODEL))


PREP_TN = 512
PREP_PLAIN = COL_GA // PREP_TN
PREP_GATES = (COL_CQ - COL_GA) // PREP_TN
PREP_TILES = N_MAIN // PREP_TN
PREP_ROW_ALIGN = 32


def _w_main_kernel(a_ref, o_ref):
    o_ref[...] = a_ref[0].astype(BF16)


def _w_main(w_in_t):
    def src_row(j):
        gates = SRC_GA + (j - PREP_PLAIN) * PREP_TN
        lora = SRC_CQ + (j - PREP_PLAIN - PREP_GATES) * PREP_TN
        row = jnp.where(j < PREP_PLAIN, j * PREP_TN, jnp.where(j < PREP_PLAIN + PREP_GATES, gates, lora))
        return pl.multiple_of(row, PREP_ROW_ALIGN)

    return pl.pallas_call(
        _w_main_kernel,
        out_shape=jax.ShapeDtypeStruct((DEPTH, N_MAIN, D_MODEL), BF16),
        grid=(DEPTH, PREP_TILES),
        in_specs=[pl.BlockSpec((pl.Element(1), pl.Element(PREP_TN), pl.Element(D_MODEL)),
                               lambda l, j: (l, src_row(j), 0))],
        out_specs=pl.BlockSpec((None, PREP_TN, D_MODEL), lambda l, j: (l, j, 0)),
        compiler_params=_params("parallel", "parallel"),
        name="w_main_layout",
    )(w_in_t)


def _inproj_kernel(x_ref, g_ref, sh_ref, sc_ref, w_ref, ws_ref, p_ref, ps_ref, h_ref):
    @pl.when(pl.program_id(1) == 0)
    def _():
        h = _rms(x_ref[...], g_ref[...]) * (1.0 + sc_ref[...]) + sh_ref[...]
        h = h.astype(BF16)
        h_ref[...] = h
        ps_ref[...] = lax.dot_general(h, ws_ref[...], NT, preferred_element_type=F32)

    p_ref[...] = lax.dot_general(h_ref[...], w_ref[...], NT, preferred_element_type=F32).astype(p_ref.dtype)


def _inproj(l, x, g, mods, mod_row, w_main, w_small, tm, tn):
    m = x.shape[0]
    return pl.pallas_call(
        _inproj_kernel,
        out_shape=[jax.ShapeDtypeStruct((m, N_MAIN), BF16), jax.ShapeDtypeStruct((m, N_SMALL), F32)],
        grid=(m // tm, N_MAIN // tn),
        in_specs=[
            pl.BlockSpec((tm, D_MODEL), lambda i, j: (i, 0)),
            _layer_vec(l, D_MODEL),
            _mod_spec(l, 0, mod_row),
            _mod_spec(l, 1, mod_row),
            _layer_spec(l, (tn, D_MODEL), lambda i, j: (j, 0)),
            _layer_spec(l, (N_SMALL, D_MODEL), lambda i, j: (0, 0)),
        ],
        out_specs=[
            pl.BlockSpec((tm, tn), lambda i, j: (i, j)),
            pl.BlockSpec((tm, N_SMALL), lambda i, j: (i, 0)),
        ],
        scratch_shapes=[pltpu.VMEM((tm, D_MODEL), BF16)],
        compiler_params=_params("parallel", "arbitrary"),
        name="in_proj",
    )(x, g, mods, mods, w_main, w_small)


GLA_GROUP = 256
GLA_BLOCK = 2 * GLA_CHUNK


def _log_sigmoid(z):
    return -(jnp.maximum(-z, 0.0) + jnp.log(1.0 + jnp.exp(-jnp.abs(z))))


def _split3(x):
    hi = x.astype(BF16)
    r1 = x - hi.astype(F32)
    mid = r1.astype(BF16)
    lo = (r1 - mid.astype(F32)).astype(BF16)
    return hi, mid, lo


def _aligned(x, m):
    return x if isinstance(x, int) else pl.multiple_of(x, m)


def _gla_kernel(qc, kc, vc, rc, sc, ql, kl, vl, rl, sl, wf, wb, bf, bb, gout,
                oc_ref, ol_ref, gf_s, gb_s, oacc, st_f, st_b):
    n_ctx, n_lat = qc.shape[0], ql.shape[0]

    r = lax.broadcasted_iota(jnp.int32, (GLA_GROUP, GLA_GROUP), 0)
    c = lax.broadcasted_iota(jnp.int32, (GLA_GROUP, GLA_GROUP), 1)
    shift = GLA_BLOCK.bit_length() - 1
    same = jnp.right_shift(r, shift) == jnp.right_shift(c, shift)
    tril = jnp.where(same & (r >= c), 1.0, 0.0).astype(BF16)
    triu = jnp.where(same & (r <= c), 1.0, 0.0).astype(BF16)

    def block_cumsum(tri, la):
        return functools.reduce(jnp.add, [_dot(tri, t) for t in _split3(la)])

    def gates(s_blk, dst):
        sb = s_blk.astype(BF16)
        la_f = _log_sigmoid(_dot(sb, wf[...]) + bf[...]) * (1.0 / GLA_TAU)
        la_b = _log_sigmoid(_dot(sb, wb[...]) + bb[...]) * (1.0 / GLA_TAU)
        gf_s[pl.ds(dst, GLA_GROUP), :] = block_cumsum(tril, la_f)
        gb_s[pl.ds(dst, GLA_GROUP), :] = block_cumsum(triu, la_b)

    for i in range(n_ctx // GLA_GROUP):
        gates(sc[pl.ds(i * GLA_GROUP, GLA_GROUP), :], i * GLA_GROUP)

    def lat_gates(i, carry):
        r0 = pl.multiple_of(i * GLA_GROUP, GLA_GROUP)
        gates(sl[pl.ds(r0, GLA_GROUP), :], n_ctx + r0)
        return carry

    lax.fori_loop(0, n_lat // GLA_GROUP, lat_gates, 0)

    cr = lax.broadcasted_iota(jnp.int32, (GLA_BLOCK, GLA_BLOCK), 0)
    cc = lax.broadcasted_iota(jnp.int32, (GLA_BLOCK, GLA_BLOCK), 1)
    half = GLA_BLOCK // 2

    def step(refs, base, n_blk, i, second_half):
        q_ref, k_ref, v_ref, r_ref, o_ref = refs
        dirs = (True, False)
        r0 = [_aligned((i if fwd else n_blk - 1 - i) * GLA_BLOCK, GLA_BLOCK) for fwd in dirs]
        rows = [pl.ds(r, GLA_BLOCK) for r in r0]
        acc_rows = [pl.ds(_aligned(base + r, GLA_BLOCK), GLA_BLOCK) for r in r0]
        g = [g_s[a, :] for g_s, a in zip((gf_s, gb_s), acc_rows)]
        g_mid = [g[0][half - 1:half, :], g[1][half:half + 1, :]]
        g_last = [g[0][GLA_BLOCK - 1:GLA_BLOCK, :], g[1][0:1, :]]
        q = [q_ref[a, :].astype(F32) * (GLA_DK ** -0.5) for a in rows]
        k = [k_ref[a, :].astype(F32) for a in rows]
        v = [v_ref[a, :] for a in rows]
        q_in = [(q[d] * jnp.exp(g[d] - g_mid[d])).astype(BF16) for d in (0, 1)]
        k_in = [(k[d] * jnp.exp(g_mid[d] - g[d])).astype(BF16) for d in (0, 1)]
        att = [lax.dot_general(q_in[d], k_in[d], NT, preferred_element_type=F32) for d in (0, 1)]
        q_st = [(q[d] * jnp.exp(g[d])).astype(BF16) for d in (0, 1)]
        k_end = [(k[d] * jnp.exp(g_last[d] - g[d])).astype(BF16) for d in (0, 1)]
        att = [jnp.where(cr >= cc, att[0], 0.0).astype(BF16), jnp.where(cr <= cc, att[1], 0.0).astype(BF16)]
        s_prev = [st_f[...], st_b[...]]
        o = [_dot(att[d], v[d]) + lax.dot_general(q_st[d], s_prev[d].astype(BF16), NT, preferred_element_type=F32)
             for d in (0, 1)]
        for d, st in enumerate((st_f, st_b)):
            st[...] = s_prev[d] * jnp.exp(g_last[d]) + lax.dot_general(v[d], k_end[d], TN,
                                                                       preferred_element_type=F32)
        for d in (0, 1):
            if second_half:
                y = _rms(oacc[acc_rows[d], :] + o[d], gout[...])
                o_ref[rows[d], :] = (y * _silu(r_ref[rows[d], :].astype(F32))).astype(o_ref.dtype)
            else:
                oacc[acc_rows[d], :] = o[d]

    st_f[...] = jnp.zeros_like(st_f)
    st_b[...] = jnp.zeros_like(st_b)
    for refs, n, base in (((qc, kc, vc, rc, oc_ref), n_ctx, 0), ((ql, kl, vl, rl, ol_ref), n_lat, n_ctx)):
        n_blk = n // GLA_BLOCK
        assert n_blk % 2 == 0
        for second_half in (False, True):
            lo = n_blk // 2 if second_half else 0
            if n_blk == 2:
                step(refs, base, n_blk, lo, second_half)
            else:
                def body(i, carry, refs=refs, base=base, n_blk=n_blk, second_half=second_half):
                    step(refs, base, n_blk, i, second_half)
                    return carry

                lax.fori_loop(lo, lo + n_blk // 2, body, 0, unroll=2)


def _gla(l, p_ctx, s_ctx, p_lat, s_lat, wf, wb, bf, bb, gout):
    def stream(n):
        return [
            pl.BlockSpec((n, GLA_DK), lambda b, h: (b, COL_Q // GLA_DK + h)),
            pl.BlockSpec((n, GLA_DK), lambda b, h: (b, COL_K // GLA_DK + h)),
            pl.BlockSpec((n, GLA_DV), lambda b, h: (b, COL_V // GLA_DV + h)),
            pl.BlockSpec((n, GLA_DV), lambda b, h: (b, COL_R // GLA_DV + h)),
            pl.BlockSpec((n, LANES), lambda b, h: (b, 0)),
        ]

    n_all = CTX_LEN + SEQ
    return pl.pallas_call(
        _gla_kernel,
        out_shape=[jax.ShapeDtypeStruct((BATCH * CTX_LEN, GLA_VW), BF16),
                   jax.ShapeDtypeStruct((BATCH * SEQ, GLA_VW), BF16)],
        grid=(BATCH, GLA_HEADS),
        in_specs=stream(CTX_LEN) + stream(SEQ) + [
            _layer_spec(l, (LANES, GLA_DK), lambda b, h: (0, h)),
            _layer_spec(l, (LANES, GLA_DK), lambda b, h: (0, h)),
            _layer_spec(l, (1, GLA_DK), lambda b, h: (0, h)),
            _layer_spec(l, (1, GLA_DK), lambda b, h: (0, h)),
            _layer_vec(l, GLA_DV),
        ],
        out_specs=[pl.BlockSpec((CTX_LEN, GLA_DV), lambda b, h: (b, h)),
                   pl.BlockSpec((SEQ, GLA_DV), lambda b, h: (b, h))],
        scratch_shapes=[
            pltpu.VMEM((n_all, GLA_DK), F32),
            pltpu.VMEM((n_all, GLA_DK), F32),
            pltpu.VMEM((n_all, GLA_DV), F32),
            pltpu.VMEM((GLA_DV, GLA_DK), F32),
            pltpu.VMEM((GLA_DV, GLA_DK), F32),
        ],
        compiler_params=_params("parallel", "parallel"),
        name="gla_bidir",
    )(p_ctx, p_ctx, p_ctx, p_ctx, s_ctx, p_lat, p_lat, p_lat, p_lat, s_lat, wf, wb, bf, bb, gout)


def _rope(x, cos, sin):
    return x * cos + pltpu.roll(x, MLA_ROPE // 2, 1) * sin


def _mla_up_kernel(cq_ref, ckv_ref, ps_ref, cos_ref, sin_ref, gq_ref, gkv_ref, wq_ref, wkv_ref,
                   q_out, k_out, vt_out):
    cqn = _rms(cq_ref[...].astype(F32), gq_ref[...]).astype(BF16)
    ckvn = _rms(ckv_ref[...].astype(F32), gkv_ref[...]).astype(BF16)
    cos, sin = cos_ref[...], sin_ref[...]
    k_rope = _rope(ps_ref[...], cos, sin).astype(BF16)
    for h in range(MLA_HEADS):
        cols = slice(h * MLA_HEAD_W, (h + 1) * MLA_HEAD_W)
        qh = _dot(cqn, wq_ref[:, cols])
        q_out[h, :, 0:MLA_NOPE] = (qh[:, :MLA_NOPE] * SCORE_SCALE_LOG2).astype(BF16)
        q_out[h, :, MLA_NOPE:] = (_rope(qh[:, MLA_NOPE:], cos, sin) * SCORE_SCALE_LOG2).astype(BF16)
        kvh = _dot(ckvn, wkv_ref[:, cols])
        k_out[h, :, 0:MLA_NOPE] = kvh[:, :MLA_NOPE].astype(BF16)
        k_out[h, :, MLA_NOPE:] = k_rope
        vt_out[h] = kvh[:, MLA_NOPE:].T.astype(BF16)


def _mla_up(l, p, ps, cos, sin, gq, gkv, wq, wkv, n_pos, tm):
    nt = n_pos // tm
    row = lambda b, i: b * nt + i
    head_spec = lambda w: pl.BlockSpec((None, MLA_HEADS, tm, w), lambda b, i: (b, 0, i, 0))
    return pl.pallas_call(
        _mla_up_kernel,
        out_shape=[jax.ShapeDtypeStruct((BATCH, MLA_HEADS, n_pos, MLA_HEAD_W), BF16),
                   jax.ShapeDtypeStruct((BATCH, MLA_HEADS, n_pos, MLA_HEAD_W), BF16),
                   jax.ShapeDtypeStruct((BATCH, MLA_HEADS, MLA_V, n_pos), BF16)],
        grid=(BATCH, nt),
        in_specs=[
            pl.BlockSpec((tm, MLA_Q_RANK), lambda b, i: (row(b, i), COL_CQ // MLA_Q_RANK)),
            pl.BlockSpec((tm, MLA_KV_RANK), lambda b, i: (row(b, i), COL_CKV // MLA_KV_RANK)),
            pl.BlockSpec((tm, LANES), lambda b, i: (row(b, i), 1)),
            pl.BlockSpec((tm, LANES), lambda b, i: (i, 0)),
            pl.BlockSpec((tm, LANES), lambda b, i: (i, 0)),
            _layer_vec(l, MLA_Q_RANK),
            _layer_vec(l, MLA_KV_RANK),
            _layer_spec(l, (MLA_Q_RANK, MLA_HEADS * MLA_HEAD_W), lambda b, i: (0, 0)),
            _layer_spec(l, (MLA_KV_RANK, MLA_HEADS * MLA_HEAD_W), lambda b, i: (0, 0)),
        ],
        out_specs=[head_spec(MLA_HEAD_W), head_spec(MLA_HEAD_W),
                   pl.BlockSpec((None, MLA_HEADS, MLA_V, tm), lambda b, i: (b, 0, 0, i))],
        compiler_params=_params("parallel", "parallel"),
        name="mla_up",
    )(p, p, ps, cos, sin, gq, gkv, wq, wkv)


ATTN_HEADS_PER_STEP = 4


def _attn_kernel(q_ref, *refs, n_kv):
    k_refs, vt_refs, o_ref = refs[:n_kv], refs[n_kv:2 * n_kv], refs[2 * n_kv]
    for h in range(ATTN_HEADS_PER_STEP):
        q = q_ref[h]
        s = [lax.dot_general(k[h], q, NT, preferred_element_type=F32) for k in k_refs]
        m = functools.reduce(jnp.maximum, [jnp.max(x, axis=0, keepdims=True) for x in s])
        p = [jnp.exp2(x - m) for x in s]
        denom = functools.reduce(jnp.add, [jnp.sum(x, axis=0, keepdims=True) for x in p])
        o = functools.reduce(jnp.add, [_dot(vt[h], x.astype(BF16)) for x, vt in zip(p, vt_refs)])
        o_ref[:, h * MLA_V:(h + 1) * MLA_V] = (o / denom).T.astype(o_ref.dtype)


def _attention(q, ks, vs, tq):
    nq = q.shape[2]
    nt = nq // tq
    hb = ATTN_HEADS_PER_STEP
    kv_spec = lambda a: pl.BlockSpec((None, hb) + a.shape[2:], lambda b, g, i: (b, g, 0, 0))
    return pl.pallas_call(
        functools.partial(_attn_kernel, n_kv=len(ks)),
        out_shape=jax.ShapeDtypeStruct((BATCH * nq, MLA_VW), BF16),
        grid=(BATCH, MLA_HEADS // hb, nt),
        in_specs=[pl.BlockSpec((None, hb, tq, MLA_HEAD_W), lambda b, g, i: (b, g, i, 0))]
        + [kv_spec(a) for a in ks] + [kv_spec(a) for a in vs],
        out_specs=pl.BlockSpec((tq, hb * MLA_V), lambda b, g, i: (b * nt + i, g)),
        compiler_params=_params("parallel", "parallel", "parallel"),
        name="mla_attn",
    )(q, *ks, *vs)


def _merge_kernel(gla_ref, mla_ref, ga_ref, gb_ref, wa_ref, wb_ref, y_ref):
    y = (jax.nn.sigmoid(ga_ref[...].astype(F32)) * _dot(gla_ref[...], wa_ref[...].astype(BF16))
         + jax.nn.sigmoid(gb_ref[...].astype(F32)) * _dot(mla_ref[...], wb_ref[...].astype(BF16)))
    y_ref[...] = y.astype(y_ref.dtype)


def _merge(l, gla, mla, p, wa, wb, tm, tn):
    m = gla.shape[0]
    return pl.pallas_call(
        _merge_kernel,
        out_shape=jax.ShapeDtypeStruct((m, D_MODEL), BF16),
        grid=(m // tm, D_MODEL // tn),
        in_specs=[
            pl.BlockSpec((tm, GLA_VW), lambda i, j: (i, 0)),
            pl.BlockSpec((tm, MLA_VW), lambda i, j: (i, 0)),
            pl.BlockSpec((tm, tn), lambda i, j: (i, COL_GA // tn + j)),
            pl.BlockSpec((tm, tn), lambda i, j: (i, COL_GB // tn + j)),
            _layer_spec(l, (GLA_VW, tn), lambda i, j: (0, j)),
            _layer_spec(l, (MLA_VW, tn), lambda i, j: (0, j)),
        ],
        out_specs=pl.BlockSpec((tm, tn), lambda i, j: (i, j)),
        compiler_params=_params("parallel", "parallel"),
        name="branch_merge",
    )(gla, mla, p, p, wa, wb)


def _proj_residual_kernel(y_ref, w_ref, x_ref, gate_ref, o_ref):
    o_ref[...] = x_ref[...] + gate_ref[...] * _dot(y_ref[...], w_ref[...].astype(BF16))


def _proj_residual(l, y, w, x, mods, which, mod_row, tm, tn):
    m, kdim = y.shape
    return pl.pallas_call(
        _proj_residual_kernel,
        out_shape=jax.ShapeDtypeStruct((m, D_MODEL), F32),
        grid=(m // tm, D_MODEL // tn),
        in_specs=[
            pl.BlockSpec((tm, kdim), lambda i, j: (i, 0)),
            _layer_spec(l, (kdim, tn), lambda i, j: (0, j)),
            pl.BlockSpec((tm, tn), lambda i, j: (i, j)),
            _mod_spec(l, which, mod_row, tn, col=True),
        ],
        out_specs=pl.BlockSpec((tm, tn), lambda i, j: (i, j)),
        compiler_params=_params("parallel", "parallel"),
        name="proj_residual",
    )(y, w, x, mods)


def _ffn_up_kernel(x_ref, g_ref, sh_ref, sc_ref, wg_ref, wu_ref, o_ref, h_ref):
    @pl.when(pl.program_id(1) == 0)
    def _():
        h = _rms(x_ref[...], g_ref[...]) * (1.0 + sc_ref[...]) + sh_ref[...]
        h_ref[...] = h.astype(BF16)

    h = h_ref[...]
    gate = _dot(h, wg_ref[...].astype(BF16))
    o_ref[...] = (_silu(gate) * _dot(h, wu_ref[...].astype(BF16))).astype(o_ref.dtype)


def _ffn_up(l, x, g, mods, mod_row, wg, wu, tm, tn):
    m = x.shape[0]
    return pl.pallas_call(
        _ffn_up_kernel,
        out_shape=jax.ShapeDtypeStruct((m, D_FF), BF16),
        grid=(m // tm, D_FF // tn),
        in_specs=[
            pl.BlockSpec((tm, D_MODEL), lambda i, j: (i, 0)),
            _layer_vec(l, D_MODEL),
            _mod_spec(l, 3, mod_row),
            _mod_spec(l, 4, mod_row),
            _layer_spec(l, (D_MODEL, tn), lambda i, j: (0, j)),
            _layer_spec(l, (D_MODEL, tn), lambda i, j: (0, j)),
        ],
        out_specs=pl.BlockSpec((tm, tn), lambda i, j: (i, j)),
        scratch_shapes=[pltpu.VMEM((tm, D_MODEL), BF16)],
        compiler_params=_params("parallel", "arbitrary"),
        name="ffn_up",
    )(x, g, mods, mods, wg, wu)


def _final_norm_kernel(x_ref, g_ref, o_ref):
    o_ref[...] = _rms(x_ref[...], g_ref[...])


def _final_norm(x, g, tm):
    m = x.shape[0]
    return pl.pallas_call(
        _final_norm_kernel,
        out_shape=jax.ShapeDtypeStruct((m, D_MODEL), F32),
        grid=(m // tm,),
        in_specs=[pl.BlockSpec((tm, D_MODEL), lambda i: (i, 0)),
                  pl.BlockSpec((1, D_MODEL), lambda i: (0, 0))],
        out_specs=pl.BlockSpec((tm, D_MODEL), lambda i: (i, 0)),
        compiler_params=_params("parallel"),
        name="final_norm",
    )(x, g)


def _deinterleave(w):
    x0, x1 = w[..., 0::2], w[..., 1::2]
    return jnp.concatenate([x0, x1, x0, x1], axis=-1)


def _rope_tables():
    rows = SEQ // GRID_W
    row = jnp.repeat(jnp.arange(rows, dtype=F32), GRID_W)
    col = jnp.tile(jnp.arange(GRID_W, dtype=F32), rows)
    n_pairs = MLA_ROPE // 4
    freqs = ROPE_BASE ** (-jnp.arange(n_pairs, dtype=F32) / n_pairs)
    ang = jnp.concatenate([row[:, None] * freqs, col[:, None] * freqs], axis=-1)
    cos, sin = jnp.cos(ang), jnp.sin(ang)
    pad = jnp.zeros((SEQ, LANES - MLA_ROPE), F32)
    cos_lat = jnp.concatenate([cos, cos, pad], axis=-1)
    sin_lat = jnp.concatenate([-sin, sin, pad], axis=-1)
    cos_ctx = jnp.concatenate([jnp.ones((CTX_LEN, MLA_ROPE), F32), jnp.zeros((CTX_LEN, LANES - MLA_ROPE), F32)], -1)
    sin_ctx = jnp.zeros((CTX_LEN, LANES), F32)
    return cos_lat, sin_lat, cos_ctx, sin_ctx


def kernel(x, c, ctx, c_ctx, w_mod, b_mod, g_attn, g_ffn, w_in, w_gla_up_f, b_gla_f, w_gla_up_b, b_gla_b, g_gla_out, g_q_lora, w_q_up, g_kv_lora, w_kv_up, w_branch_a, w_branch_b, w_out, w_ffn_gate, w_ffn_up, w_ffn_down, g_final):
    L = DEPTH
    w_in_t = jnp.swapaxes(w_in, 1, 2)
    w_main = _w_main(w_in_t)
    k_rope_rows = w_in_t[:, SRC_KR:SRC_GA, :]
    x0, x1 = k_rope_rows[:, 0::2, :], k_rope_rows[:, 1::2, :]
    w_small = jnp.concatenate(
        [w_in_t[:, SRC_GATE:SRC_CQ, :], jnp.zeros((L, LANES - 2 * GLA_RANK, D_MODEL), F32), x0, x1, x0, x1],
        axis=1).astype(BF16)
    wq = w_q_up.reshape(L, MLA_Q_RANK, MLA_HEADS, MLA_QK)
    wq = jnp.concatenate([wq[..., :MLA_NOPE], _deinterleave(wq[..., MLA_NOPE:])], axis=-1)
    wq = wq.reshape(L, MLA_Q_RANK, MLA_HEADS * MLA_HEAD_W).astype(BF16)
    wkv = w_kv_up.astype(BF16)
    zpad = lambda n: jnp.zeros((L, n, GLA_QK), F32)
    wup_f = jnp.concatenate([w_gla_up_f, zpad(LANES - GLA_RANK)], axis=1).astype(BF16)
    wup_b = jnp.concatenate([zpad(GLA_RANK), w_gla_up_b, zpad(LANES - 2 * GLA_RANK)], axis=1).astype(BF16)
    wo, wd = w_out.astype(BF16), w_ffn_down.astype(BF16)
    cos_lat, sin_lat, cos_ctx, sin_ctx = _rope_tables()
    vec = lambda v: v.reshape(L, 1, -1)
    g_attn, g_ffn, g_gla_out, g_q_lora, g_kv_lora = map(vec, (g_attn, g_ffn, g_gla_out, g_q_lora, g_kv_lora))
    b_gla_f, b_gla_b = vec(b_gla_f), vec(b_gla_b)

    cs = jnp.concatenate([c, c_ctx[None, :], jnp.zeros((8 - BATCH - 1, D_MODEL), F32)], axis=0)
    mods = _modulation(cs, w_mod, b_mod).reshape(L, 8, 6, 1, D_MODEL)

    xl = x.reshape(BATCH * SEQ, D_MODEL)
    xc = ctx.reshape(BATCH * CTX_LEN, D_MODEL)
    tm = 1024
    lat_row = lambda i: i // (SEQ // tm)
    ctx_row = lambda i: BATCH

    for l in range(L):
        last = l == L - 1
        p_lat, s_lat = _inproj(l, xl, g_attn, mods, lat_row, w_main, w_small, tm, 1024)
        p_ctx, s_ctx = _inproj(l, xc, g_attn, mods, ctx_row, w_main, w_small, tm, 1024)

        gla_ctx, gla_lat = _gla(l, p_ctx, s_ctx, p_lat, s_lat, wup_f, wup_b, b_gla_f, b_gla_b, g_gla_out)

        q_lat, k_lat, v_lat = _mla_up(l, p_lat, s_lat, cos_lat, sin_lat, g_q_lora, g_kv_lora, wq, wkv, SEQ, 512)
        q_ctx, k_ctx, v_ctx = _mla_up(l, p_ctx, s_ctx, cos_ctx, sin_ctx, g_q_lora, g_kv_lora, wq, wkv,
                                      CTX_LEN, CTX_LEN)
        mla_lat = _attention(q_lat, [k_ctx, k_lat], [v_ctx, v_lat], 512)

        y = _merge(l, gla_lat, mla_lat, p_lat, w_branch_a, w_branch_b, tm, 512)
        xl = _proj_residual(l, y, wo, xl, mods, 2, lat_row, tm, 512)
        if not last:
            mla_ctx = _attention(q_ctx, [k_ctx], [v_ctx], CTX_LEN)
            y = _merge(l, gla_ctx, mla_ctx, p_ctx, w_branch_a, w_branch_b, tm, 512)
            xc = _proj_residual(l, y, wo, xc, mods, 2, ctx_row, tm, 512)

        hid = _ffn_up(l, xl, g_ffn, mods, lat_row, w_ffn_gate, w_ffn_up, tm, 512)
        xl = _proj_residual(l, hid, wd, xl, mods, 5, lat_row, tm, 512)
        if not last:
            hid = _ffn_up(l, xc, g_ffn, mods, ctx_row, w_ffn_gate, w_ffn_up, tm, 512)
            xc = _proj_residual(l, hid, wd, xc, mods, 5, ctx_row, tm, 512)

    return _final_norm(xl, g_final.reshape(1, -1), 512).reshape(BATCH, SEQ, D_MODEL)
```
